```python
import math
import jax, jax.numpy as jnp
from jax import lax
import numpy as np

D_MODEL = 1024
BATCH = 8
SEQ = 4096
DEPTH = 4

N_DIFF_HEADS = 4
DIFF_HEAD_DIM = 64
DIFF_V_DIM = 2 * DIFF_HEAD_DIM
QK_COLS = N_DIFF_HEADS * 2 * DIFF_HEAD_DIM
DIFF_WIDTH = N_DIFF_HEADS * DIFF_V_DIM
ROT_DIM = DIFF_HEAD_DIM // 4
ROPE_THETA = 500000.0
Q_BLOCK = 128
GMLP_GROUPS = 4
GMLP_GROUP_DIM = 128
GMLP_WIDTH = GMLP_GROUPS * GMLP_GROUP_DIM
CHUNK = 128
N_BRANCH = 2
IN_COLS = 2 * QK_COLS + DIFF_WIDTH + 2 * GMLP_WIDTH + N_BRANCH * D_MODEL
D_FF = 2816
CONV_WIDTH = 3
LN_EPS = 1e-5
RMS_EPS = 1e-5
DN_ALPHA = (2 * DEPTH) ** 0.25
DN_BETA = (8 * DEPTH) ** -0.25

kernel_name = "hybrid_diffattn_gmlp_deepnorm_encoder"


def layer_norm(t, g, b):
    tf = t.astype(jnp.float32)
    mu = jnp.mean(tf, axis=-1, keepdims=True)
    var = jnp.mean(jnp.square(tf - mu), axis=-1, keepdims=True)
    return ((tf - mu) * lax.rsqrt(var + LN_EPS) * g + b).astype(t.dtype)


def rms_norm(t, g):
    tf = t.astype(jnp.float32)
    ms = jnp.mean(jnp.square(tf), axis=-1, keepdims=True)
    return (tf * lax.rsqrt(ms + RMS_EPS) * g).astype(t.dtype)


def rope_tables(seq):
    inv = 1.0 / (ROPE_THETA ** (jnp.arange(0, ROT_DIM, 2, dtype=jnp.float32) / ROT_DIM))
    ang = jnp.arange(seq, dtype=jnp.float32)[:, None] * inv[None, :]
    return jnp.cos(ang), jnp.sin(ang)


def apply_partial_rope(t, cos, sin):
    c = cos[None, :, None, None, :].astype(t.dtype)
    s = sin[None, :, None, None, :].astype(t.dtype)
    r1 = t[..., : ROT_DIM // 2]
    r2 = t[..., ROT_DIM // 2: ROT_DIM]
    rest = t[..., ROT_DIM:]
    return jnp.concatenate([r1 * c - r2 * s, r2 * c + r1 * s, rest], axis=-1)


def diff_attention(q, k, v, lam, sub_g, lam_init):
    B, S, H, _, dh = q.shape
    nb = S // Q_BLOCK
    qb = q.reshape(B, nb, Q_BLOCK, H, 2, dh).transpose(1, 0, 3, 4, 2, 5)
    kt = k.transpose(0, 2, 3, 1, 4)
    vt = v.transpose(0, 2, 1, 3)
    scale = dh ** -0.5

    def one_block(qblk):
        s = jnp.einsum('bhcqd,bhckd->bhcqk', qblk, kt).astype(jnp.float32) * scale
        p = jax.nn.softmax(s, axis=-1)
        a = (p[:, :, 0] - lam * p[:, :, 1]).astype(vt.dtype)
        return jnp.einsum('bhqk,bhkd->bhqd', a, vt)

    o = lax.map(one_block, qb)
    o = o.transpose(1, 0, 3, 2, 4).reshape(B, S, H, 2 * dh)
    o = rms_norm(o, sub_g) * (1.0 - lam_init)
    return o.reshape(B, S, H * 2 * dh)


def spatial_gating(u, vg, ln_g, ln_b, w_s, b_s):
    B, S, _ = vg.shape
    vg = layer_norm(vg, ln_g, ln_b)
    vc = vg.reshape(B, S // CHUNK, CHUNK, GMLP_GROUPS, GMLP_GROUP_DIM)
    z = jnp.einsum('gts,bnsgc->bntgc', w_s, vc) + b_s.T[None, None, :, :, None]
    return u * z.reshape(B, S, GMLP_WIDTH)


def conv_gated_ffn(x, w_up, conv_w, conv_b, w_down):
    h = x @ w_up
    a, b = jnp.split(h, 2, axis=-1)
    a = lax.conv_general_dilated(
        a, conv_w[:, None, :].astype(a.dtype), window_strides=(1,),
        padding=[(CONV_WIDTH // 2, CONV_WIDTH // 2)],
        dimension_numbers=('NWC', 'WIO', 'NWC'), feature_group_count=D_FF) + conv_b
    return (jax.nn.gelu(a) * b) @ w_down


def setup_inputs(seed: int = 0) -> dict:
    key = jax.random.key(seed)
    ks = jax.random.split(key, 32)
    f32 = jnp.float32
    nrm = lambda k, shape, s: jax.random.normal(k, shape, f32) * s
    L, D = DEPTH, D_MODEL
    return {
        "x": nrm(ks[0], (BATCH, SEQ, D), 1.0),
        "ln_in_g": 1.0 + nrm(ks[1], (D,), 0.02),
        "ln_in_b": nrm(ks[2], (D,), 0.02),
        "w_in": nrm(ks[3], (L, D, IN_COLS), D ** -0.5),
        "b_gate": nrm(ks[4], (L, N_BRANCH, D), 0.02),
        "lambda_q1": nrm(ks[5], (L, DIFF_HEAD_DIM), 0.1),
        "lambda_k1": nrm(ks[6], (L, DIFF_HEAD_DIM), 0.1),
        "lambda_q2": nrm(ks[7], (L, DIFF_HEAD_DIM), 0.1),
        "lambda_k2": nrm(ks[8], (L, DIFF_HEAD_DIM), 0.1),
        "subln_g": 1.0 + nrm(ks[9], (L, DIFF_V_DIM), 0.02),
        "gmlp_ln_g": 1.0 + nrm(ks[10], (L, GMLP_WIDTH), 0.02),
        "gmlp_ln_b": nrm(ks[11], (L, GMLP_WIDTH), 0.02),
        "w_spatial": nrm(ks[12], (L, GMLP_GROUPS, CHUNK, CHUNK), CHUNK ** -0.5),
        "b_spatial": 1.0 + nrm(ks[13], (L, GMLP_GROUPS, CHUNK), 0.02),
        "p_attn": nrm(ks[14], (L, DIFF_WIDTH, D), DIFF_WIDTH ** -0.5),
        "p_gmlp": nrm(ks[15], (L, GMLP_WIDTH, D), GMLP_WIDTH ** -0.5),
        "w_o": nrm(ks[16], (L, D, D), DN_BETA * D ** -0.5),
        "ln1_g": 1.0 + nrm(ks[17], (L, D), 0.02),
        "ln1_b": nrm(ks[18], (L, D), 0.02),
        "w_up": nrm(ks[19], (L, D, 2 * D_FF), D ** -0.5),
        "conv_w": nrm(ks[20], (L, CONV_WIDTH, D_FF), CONV_WIDTH ** -0.5),
        "conv_b": nrm(ks[21], (L, D_FF), 0.02),
        "w_down": nrm(ks[22], (L, D_FF, D), DN_BETA * D_FF ** -0.5),
        "ln2_g": 1.0 + nrm(ks[23], (L, D), 0.02),
        "ln2_b": nrm(ks[24], (L, D), 0.02),
    }


def reference(x, ln_in_g, ln_in_b, w_in, b_gate, lambda_q1, lambda_k1, lambda_q2,
              lambda_k2, subln_g, gmlp_ln_g, gmlp_ln_b, w_spatial, b_spatial,
              p_attn, p_gmlp, w_o, ln1_g, ln1_b, w_up, conv_w, conv_b, w_down,
              ln2_g, ln2_b):
    B, S, D = x.shape
    cos, sin = rope_tables(S)
    splits = [QK_COLS, 2 * QK_COLS, 2 * QK_COLS + DIFF_WIDTH,
              2 * QK_COLS + DIFF_WIDTH + GMLP_WIDTH,
              2 * QK_COLS + DIFF_WIDTH + 2 * GMLP_WIDTH]
    x = layer_norm(x, ln_in_g, ln_in_b)
    for l in range(DEPTH):
        lam_init = 0.8 - 0.6 * math.exp(-0.3 * l)
        h = x @ w_in[l]
        q, k, v, u, vg, gl = jnp.split(h, splits, axis=-1)
        q = apply_partial_rope(q.reshape(B, S, N_DIFF_HEADS, 2, DIFF_HEAD_DIM), cos, sin)
        k = apply_partial_rope(k.reshape(B, S, N_DIFF_HEADS, 2, DIFF_HEAD_DIM), cos, sin)
        v = v.reshape(B, S, N_DIFF_HEADS, DIFF_V_DIM)
        lam = (jnp.exp(jnp.sum(lambda_q1[l].astype(jnp.float32) * lambda_k1[l].astype(jnp.float32)))
               - jnp.exp(jnp.sum(lambda_q2[l].astype(jnp.float32) * lambda_k2[l].astype(jnp.float32)))
               + lam_init)
        a_out = diff_attention(q, k, v, lam, subln_g[l], lam_init)
        g_out = spatial_gating(u, vg, gmlp_ln_g[l], gmlp_ln_b[l],
                               w_spatial[l], b_spatial[l])
        gates = jax.nn.sigmoid((gl + b_gate[l].reshape(-1)).astype(jnp.float32))
        gates = gates.astype(x.dtype).reshape(B, S, N_BRANCH, D)
        mixed = gates[:, :, 0] * (a_out @ p_attn[l]) + gates[:, :, 1] * (g_out @ p_gmlp[l])
        x = layer_norm(DN_ALPHA * x + mixed @ w_o[l], ln1_g[l], ln1_b[l])
        y = conv_gated_ffn(x, w_up[l], conv_w[l], conv_b[l], w_down[l])
        x = layer_norm(DN_ALPHA * x + y, ln2_g[l], ln2_b[l])
    return x
```

```python
import functools
import math

import jax
import jax.numpy as jnp
from jax import lax
from jax.experimental import pallas as pl
from jax.experimental.pallas import tpu as pltpu

F32 = jnp.float32
BF16 = jnp.bfloat16

D_MODEL = 1024
DEPTH = 4
N_HEADS = 4
HEAD_DIM = 64
V_DIM = 2 * HEAD_DIM
QK_COLS = N_HEADS * 2 * HEAD_DIM
DIFF_WIDTH = N_HEADS * V_DIM
ROT_DIM = HEAD_DIM // 4
ROPE_THETA = 500000.0
GROUPS = 4
GROUP_DIM = 128
GMLP_WIDTH = GROUPS * GROUP_DIM
CHUNK = 128
N_BRANCH = 2
IN_COLS = 2 * QK_COLS + DIFF_WIDTH + 2 * GMLP_WIDTH + N_BRANCH * D_MODEL
D_FF = 2816
LN_EPS = 1e-5
RMS_EPS = 1e-5
DN_ALPHA = (2 * DEPTH) ** 0.25

LANES = 128
BF16_ROWS = 16
VMEM_LIMIT = 56 * 1024 * 1024

TM_LN = 512
TM_IN = 512
Q_BLOCK = 256
TM_MERGE = 512
TM_FFN = 512
TF = 256
NF = D_FF // TF
HALO = BF16_ROWS

QK_SCALE = HEAD_DIM ** -0.5 * math.log2(math.e)


def _ln(t, g, b):
    mu = jnp.mean(t, axis=-1, keepdims=True)
    d = t - mu
    var = jnp.mean(d * d, axis=-1, keepdims=True)
    return d * lax.rsqrt(var + LN_EPS) * g + b


def _params(n_axes):
    return pltpu.CompilerParams(dimension_semantics=("arbitrary",) * n_axes,
                                vmem_limit_bytes=VMEM_LIMIT)


def _ln_kernel(x_ref, g_ref, b_ref, o_ref):
    o_ref[0] = _ln(x_ref[0], g_ref[...], b_ref[...])


def _input_ln(x, g, b):
    B, S, D = x.shape
    return pl.pallas_call(
        _ln_kernel,
        grid=(B, S // TM_LN),
        in_specs=[pl.BlockSpec((1, TM_LN, D), lambda b, i: (b, i, 0)),
                  pl.BlockSpec((1, D), lambda b, i: (0, 0)),
                  pl.BlockSpec((1, D), lambda b, i: (0, 0))],
        out_specs=pl.BlockSpec((1, TM_LN, D), lambda b, i: (b, i, 0)),
        out_shape=jax.ShapeDtypeStruct(x.shape, F32),
        compiler_params=_params(2),
        name="input_ln",
    )(x, g.reshape(1, D), b.reshape(1, D))


def _rope(t, c, s_next, s_prev):
    return (t * c + pltpu.roll(t, LANES - ROT_DIM // 2, axis=1) * s_next
            + pltpu.roll(t, ROT_DIM // 2, axis=1) * s_prev)


def _inproj_kernel(x_ref, w_ref, rc_ref, rn_ref, rp_ref, gg_ref, gb_ref, bg_ref,
                   q_ref, k_ref, v_ref, u_ref, vg_ref, gate_ref):
    xb = x_ref[0].astype(BF16)
    c, s_next, s_prev = rc_ref[...], rn_ref[...], rp_ref[...]

    def proj(lo, width):
        return jnp.dot(xb, w_ref[:, lo:lo + width], preferred_element_type=F32)

    for h in range(N_HEADS):
        q = proj(h * LANES, LANES)
        q_ref[0, :, h * LANES:(h + 1) * LANES] = (_rope(q, c, s_next, s_prev) * QK_SCALE).astype(BF16)
        k = proj(QK_COLS + h * LANES, LANES)
        k_ref[0, :, h * LANES:(h + 1) * LANES] = _rope(k, c, s_next, s_prev).astype(BF16)
    off = 2 * QK_COLS
    v_ref[0] = proj(off, DIFF_WIDTH).astype(BF16)
    off += DIFF_WIDTH
    u_ref[0] = proj(off, GMLP_WIDTH).astype(BF16)
    off += GMLP_WIDTH
    vg_ref[0] = _ln(proj(off, GMLP_WIDTH), gg_ref[...], gb_ref[...]).astype(BF16)
    off += GMLP_WIDTH
    for j in range(N_BRANCH * D_MODEL // 512):
        z = proj(off + j * 512, 512) + bg_ref[:, j * 512:(j + 1) * 512]
        gate_ref[0, :, j * 512:(j + 1) * 512] = (1.0 / (1.0 + jnp.exp(-z))).astype(BF16)


def _inproj(x, w_in, l, rope_c, rope_n, rope_p, gg, gb, bg):
    B, S, D = x.shape
    tm = TM_IN
    tok = lambda width: pl.BlockSpec((1, tm, width), lambda b, i: (b, i, 0))
    lay = lambda width: pl.BlockSpec((None, 1, width), lambda b, i: (l, 0, 0))
    rope = pl.BlockSpec((tm, LANES), lambda b, i: (i, 0))
    outs = [jax.ShapeDtypeStruct((B, S, w), BF16)
            for w in (QK_COLS, QK_COLS, DIFF_WIDTH, GMLP_WIDTH, GMLP_WIDTH, N_BRANCH * D)]
    return pl.pallas_call(
        _inproj_kernel,
        grid=(B, S // tm),
        in_specs=[tok(D),
                  pl.BlockSpec((None, D, IN_COLS), lambda b, i: (l, 0, 0)),
                  rope, rope, rope,
                  lay(GMLP_WIDTH), lay(GMLP_WIDTH), lay(N_BRANCH * D)],
        out_specs=[tok(QK_COLS), tok(QK_COLS), tok(DIFF_WIDTH), tok(GMLP_WIDTH),
                   tok(GMLP_WIDTH), tok(N_BRANCH * D)],
        out_shape=outs,
        compiler_params=_params(2),
        name="inproj",
    )(x, w_in, rope_c, rope_n, rope_p, gg, gb, bg)


def _attn_kernel(lam_init, q_ref, k_ref, v_ref, lq1_ref, lk1_ref, lq2_ref, lk2_ref, sg_ref,
                 o_ref):
    qb = Q_BLOCK
    lam = (jnp.exp(jnp.sum(lq1_ref[...] * lk1_ref[...], axis=-1, keepdims=True))
           - jnp.exp(jnp.sum(lq2_ref[...] * lk2_ref[...], axis=-1, keepdims=True))
           + lam_init)
    q = q_ref[0]
    lane = lax.broadcasted_iota(jnp.int32, q.shape, 1)
    zero = jnp.zeros_like(q)
    qs = jnp.concatenate([jnp.where(lane < HEAD_DIM, q, zero),
                          jnp.where(lane >= HEAD_DIM, q, zero)], axis=0)
    s = lax.dot_general(qs, k_ref[0], (((1,), (1,)), ((), ())),
                        preferred_element_type=F32)
    m = jnp.max(s, axis=-1, keepdims=True)
    e = jnp.exp2(s - m)
    l = jnp.sum(e, axis=-1, keepdims=True)
    l1, l2 = l[:qb], l[qb:]
    a = (e[:qb] - (lam * l1 / l2) * e[qb:]).astype(BF16)
    o = jnp.dot(a, v_ref[0], preferred_element_type=F32) / l1
    ms = jnp.mean(o * o, axis=-1, keepdims=True)
    o_ref[0] = (o * lax.rsqrt(ms + RMS_EPS) * sg_ref[...] * (1.0 - lam_init)).astype(BF16)


def _attention(q, k, v, l, lam_init, lq1, lk1, lq2, lk2, sg):
    B, S, _ = q.shape
    lay = lambda width: pl.BlockSpec((None, 1, width), lambda b, h, i: (l, 0, 0))
    return pl.pallas_call(
        functools.partial(_attn_kernel, lam_init),
        grid=(B, N_HEADS, S // Q_BLOCK),
        in_specs=[pl.BlockSpec((1, Q_BLOCK, LANES), lambda b, h, i: (b, i, h)),
                  pl.BlockSpec((1, S, LANES), lambda b, h, i: (b, 0, h)),
                  pl.BlockSpec((1, S, V_DIM), lambda b, h, i: (b, 0, h)),
                  lay(HEAD_DIM), lay(HEAD_DIM), lay(HEAD_DIM), lay(HEAD_DIM), lay(V_DIM)],
        out_specs=pl.BlockSpec((1, Q_BLOCK, V_DIM), lambda b, h, i: (b, i, h)),
        out_shape=jax.ShapeDtypeStruct((B, S, DIFF_WIDTH), BF16),
        compiler_params=_params(3),
        name="diff_attn",
    )(q, k, v, lq1, lk1, lq2, lk2, sg)


def _merge_kernel(x_ref, a_ref, u_ref, vg_ref, gate_ref, ws_ref, bs_ref, pa_ref, pg_ref,
                  wo_ref, g_ref, b_ref, o_ref, gout_ref):
    for n in range(TM_MERGE // CHUNK):
        rows = slice(n * CHUNK, (n + 1) * CHUNK)
        for g in range(GROUPS):
            cols = slice(g * GROUP_DIM, (g + 1) * GROUP_DIM)
            z = jnp.dot(ws_ref[g], vg_ref[0, rows, cols], preferred_element_type=F32) + bs_ref[g]
            gout_ref[rows, cols] = (u_ref[0, rows, cols].astype(F32) * z).astype(BF16)
    ma = jnp.dot(a_ref[0], pa_ref[...], preferred_element_type=F32)
    mg = jnp.dot(gout_ref[...], pg_ref[...], preferred_element_type=F32)
    mixed = (gate_ref[0, :, :D_MODEL].astype(F32) * ma
             + gate_ref[0, :, D_MODEL:].astype(F32) * mg)
    y = jnp.dot(mixed.astype(BF16), wo_ref[...], preferred_element_type=F32)
    o_ref[0] = _ln(DN_ALPHA * x_ref[0] + y, g_ref[...], b_ref[...])


def _merge(x, a, u, vg, gates, l, ws, bs, pa, pg, wo, g, b):
    B, S, D = x.shape
    tm = TM_MERGE
    tok = lambda width: pl.BlockSpec((1, tm, width), lambda bi, i: (bi, i, 0))
    lay = lambda *shape: pl.BlockSpec((None,) + shape, lambda bi, i: (l,) + (0,) * len(shape))
    return pl.pallas_call(
        _merge_kernel,
        grid=(B, S // tm),
        in_specs=[tok(D), tok(DIFF_WIDTH), tok(GMLP_WIDTH), tok(GMLP_WIDTH), tok(N_BRANCH * D),
                  lay(GROUPS, CHUNK, CHUNK), lay(GROUPS, CHUNK, GROUP_DIM),
                  lay(DIFF_WIDTH, D), lay(GMLP_WIDTH, D), lay(D, D), lay(1, D), lay(1, D)],
        out_specs=tok(D),
        out_shape=jax.ShapeDtypeStruct(x.shape, F32),
        scratch_shapes=[pltpu.VMEM((tm, GMLP_WIDTH), BF16)],
        compiler_params=_params(2),
        name="merge",
    )(x, a, u, vg, gates, ws, bs, pa, pg, wo, g, b)


def _gelu_tanh(t):
    return 0.5 * t * (1.0 + jnp.tanh(math.sqrt(2.0 / math.pi) * (t + 0.044715 * (t * t * t))))


def _ffn_kernel(x_ref, xp_ref, xn_ref, wa_ref, wb_ref, cw_ref, cb_ref, wd_ref, g_ref, b_ref,
                o_ref, xs_ref, a_ref, acc_ref):
    tm = TM_FFN
    i = pl.program_id(1)
    xs_ref[0:HALO] = jnp.where(i > 0, xp_ref[0], 0.0).astype(BF16)
    xs_ref[HALO:HALO + tm] = x_ref[0].astype(BF16)
    xs_ref[HALO + tm:] = jnp.where(i < pl.num_programs(1) - 1, xn_ref[0], 0.0).astype(BF16)
    acc_ref[...] = jnp.zeros_like(acc_ref)

    def step(j, carry):
        a_ref[...] = jnp.dot(xs_ref[...], wa_ref[j], preferred_element_type=F32)
        gate = jnp.dot(xs_ref[HALO:HALO + tm], wb_ref[j], preferred_element_type=F32)
        cw = cw_ref[j]
        conv = (a_ref[HALO - 1:HALO - 1 + tm] * cw[0:1]
                + a_ref[HALO:HALO + tm] * cw[1:2]
                + a_ref[HALO + 1:HALO + 1 + tm] * cw[2:3] + cb_ref[j])
        hidden = (_gelu_tanh(conv) * gate).astype(BF16)
        acc_ref[...] += jnp.dot(hidden, wd_ref[j], preferred_element_type=F32)
        return carry

    lax.fori_loop(0, NF, step, 0)
    o_ref[0] = _ln(DN_ALPHA * x_ref[0] + acc_ref[...], g_ref[...], b_ref[...])


def _ffn(x, l, wa, wb, cw, cb, wd, g, b):
    B, S, D = x.shape
    tm = TM_FFN
    nh = tm // HALO
    last = S // HALO - 1
    lay = lambda *shape: pl.BlockSpec((None,) + shape, lambda bi, i: (l,) + (0,) * len(shape))
    return pl.pallas_call(
        _ffn_kernel,
        grid=(B, S // tm),
        in_specs=[pl.BlockSpec((1, tm, D), lambda bi, i: (bi, i, 0)),
                  pl.BlockSpec((1, HALO, D), lambda bi, i: (bi, jnp.maximum(i * nh - 1, 0), 0)),
                  pl.BlockSpec((1, HALO, D), lambda bi, i: (bi, jnp.minimum((i + 1) * nh, last), 0)),
                  lay(NF, D, TF), lay(NF, D, TF), lay(NF, 3, TF), lay(NF, 1, TF),
                  lay(NF, TF, D), lay(1, D), lay(1, D)],
        out_specs=pl.BlockSpec((1, tm, D), lambda bi, i: (bi, i, 0)),
        out_shape=jax.ShapeDtypeStruct(x.shape, F32),
        scratch_shapes=[pltpu.VMEM((tm + 2 * HALO, D), BF16),
                        pltpu.VMEM((tm + 2 * HALO, TF), F32),
                        pltpu.VMEM((tm, D), F32)],
        compiler_params=_params(2),
        name="ffn",
    )(x, x, x, wa, wb, cw, cb, wd, g, b)


def _rope_tables(seq):
    inv = 1.0 / (ROPE_THETA ** (jnp.arange(0, ROT_DIM, 2, dtype=F32) / ROT_DIM))
    ang = jnp.arange(seq, dtype=F32)[:, None] * inv[None, :]
    cos, sin = jnp.cos(ang), jnp.sin(ang)
    half = ROT_DIM // 2
    pad = HEAD_DIM - ROT_DIM
    ones = jnp.ones((seq, pad), F32)
    zeros = jnp.zeros((seq, pad), F32)
    zh = jnp.zeros((seq, half), F32)
    c = jnp.concatenate([cos, cos, ones], axis=1)
    s_next = jnp.concatenate([-sin, zh, zeros], axis=1)
    s_prev = jnp.concatenate([zh, sin, zeros], axis=1)
    rep = LANES // HEAD_DIM
    return tuple(jnp.tile(t, (1, rep)) for t in (c, s_next, s_prev))


def kernel(x, ln_in_g, ln_in_b, w_in, b_gate, lambda_q1, lambda_k1, lambda_q2, lambda_k2, subln_g, gmlp_ln_g, gmlp_ln_b, w_spatial, b_spatial, p_attn, p_gmlp, w_o, ln1_g, ln1_b, w_up, conv_w, conv_b, w_down, ln2_g, ln2_b):
    B, S, D = x.shape
    L = DEPTH
    rope_c, rope_n, rope_p = _rope_tables(S)
    row = lambda t: t.reshape(L, 1, -1)
    w_in_b = w_in.astype(BF16)
    ws_b = w_spatial.astype(BF16)
    bs_full = jnp.broadcast_to(b_spatial[..., None], (L, GROUPS, CHUNK, GROUP_DIM))
    pa_b, pg_b, wo_b = p_attn.astype(BF16), p_gmlp.astype(BF16), w_o.astype(BF16)
    split_up = lambda w: w.reshape(L, D, NF, TF).transpose(0, 2, 1, 3).astype(BF16)
    wa_b, wb_b = split_up(w_up[:, :, :D_FF]), split_up(w_up[:, :, D_FF:])
    wd_b = w_down.reshape(L, NF, TF, D).astype(BF16)
    cw_t = conv_w.reshape(L, 3, NF, TF).transpose(0, 2, 1, 3)
    cb_t = conv_b.reshape(L, NF, 1, TF)

    x = _input_ln(x, ln_in_g, ln_in_b)
    for l in range(L):
        lam_init = 0.8 - 0.6 * math.exp(-0.3 * l)
        q, k, v, u, vg, gates = _inproj(x, w_in_b, l, rope_c, rope_n, rope_p,
                                        row(gmlp_ln_g), row(gmlp_ln_b), row(b_gate))
        a = _attention(q, k, v, l, lam_init, row(lambda_q1), row(lambda_k1),
                       row(lambda_q2), row(lambda_k2), row(subln_g))
        x = _merge(x, a, u, vg, gates, l, ws_b, bs_full, pa_b, pg_b, wo_b,
                   row(ln1_g), row(ln1_b))
        x = _ffn(x, l, wa_b, wb_b, cw_t, cb_t, wd_b, row(ln2_g), row(ln2_b))
    return x
```

```python
import functools
import math

import jax
import jax.numpy as jnp
from jax import lax
from jax.experimental import pallas as pl
from jax.experimental.pallas import tpu as pltpu

F32 = jnp.float32
BF16 = jnp.bfloat16

D_MODEL = 1024
DEPTH = 4
N_HEADS = 4
HEAD_DIM = 64
V_DIM = 2 * HEAD_DIM
QK_COLS = N_HEADS * 2 * HEAD_DIM
DIFF_WIDTH = N_HEADS * V_DIM
ROT_DIM = HEAD_DIM // 4
ROPE_THETA = 500000.0
GROUPS = 4
GROUP_DIM = 128
GMLP_WIDTH = GROUPS * GROUP_DIM
CHUNK = 128
N_BRANCH = 2
IN_COLS = 2 * QK_COLS + DIFF_WIDTH + 2 * GMLP_WIDTH + N_BRANCH * D_MODEL
D_FF = 2816
LN_EPS = 1e-5
RMS_EPS = 1e-5
DN_ALPHA = (2 * DEPTH) ** 0.25

LANES = 128
BF16_ROWS = 16
VMEM_LIMIT = 56 * 1024 * 1024

TM_LN = 512
TM_IN = 512
Q_BLOCK = 256
K_TILE = 256
VT_CHUNK = 512
SCORES_LAG = 2
EXP_LAG = 3
TM_MERGE = 512
TM_FFN = 512
TF = 256
NF = D_FF // TF
HALO = BF16_ROWS

QK_SCALE = HEAD_DIM ** -0.5 * math.log2(math.e)


def _ln(t, g, b):
    mu = jnp.mean(t, axis=-1, keepdims=True)
    d = t - mu
    var = jnp.mean(d * d, axis=-1, keepdims=True)
    return d * lax.rsqrt(var + LN_EPS) * g + b


def _params(n_axes):
    return pltpu.CompilerParams(dimension_semantics=("arbitrary",) * n_axes,
                                vmem_limit_bytes=VMEM_LIMIT)


def _ln_kernel(x_ref, g_ref, b_ref, o_ref):
    o_ref[0] = _ln(x_ref[0], g_ref[...], b_ref[...])


def _input_ln(x, g, b):
    B, S, D = x.shape
    return pl.pallas_call(
        _ln_kernel,
        grid=(B, S // TM_LN),
        in_specs=[pl.BlockSpec((1, TM_LN, D), lambda b, i: (b, i, 0)),
                  pl.BlockSpec((1, D), lambda b, i: (0, 0)),
                  pl.BlockSpec((1, D), lambda b, i: (0, 0))],
        out_specs=pl.BlockSpec((1, TM_LN, D), lambda b, i: (b, i, 0)),
        out_shape=jax.ShapeDtypeStruct(x.shape, F32),
        compiler_params=_params(2),
        name="input_ln",
    )(x, g.reshape(1, D), b.reshape(1, D))


def _rope(t, c, s_next, s_prev):
    return (t * c + pltpu.roll(t, LANES - ROT_DIM // 2, axis=1) * s_next
            + pltpu.roll(t, ROT_DIM // 2, axis=1) * s_prev)


def _inproj_kernel(x_ref, w_ref, rc_ref, rn_ref, rp_ref, gg_ref, gb_ref, bg_ref,
                   q_ref, k_ref, v_ref, u_ref, vg_ref, gate_ref):
    xb = x_ref[0].astype(BF16)
    c, s_next, s_prev = rc_ref[...], rn_ref[...], rp_ref[...]

    def proj(lo, width):
        return jnp.dot(xb, w_ref[:, lo:lo + width], preferred_element_type=F32)

    q = proj(0, QK_COLS)
    k = proj(QK_COLS, QK_COLS)
    for h in range(N_HEADS):
        cols = slice(h * LANES, (h + 1) * LANES)
        q_ref[0, :, cols] = (_rope(q[:, cols], c, s_next, s_prev) * QK_SCALE).astype(BF16)
        k_ref[0, :, cols] = _rope(k[:, cols], c, s_next, s_prev).astype(BF16)
    off = 2 * QK_COLS
    v_ref[0] = proj(off, DIFF_WIDTH).astype(BF16)
    off += DIFF_WIDTH
    u_ref[0] = proj(off, GMLP_WIDTH).astype(BF16)
    off += GMLP_WIDTH
    vg_ref[0] = _ln(proj(off, GMLP_WIDTH), gg_ref[...], gb_ref[...]).astype(BF16)
    off += GMLP_WIDTH
    for j in range(N_BRANCH * D_MODEL // 512):
        z = proj(off + j * 512, 512) + bg_ref[:, j * 512:(j + 1) * 512]
        gate_ref[0, :, j * 512:(j + 1) * 512] = (1.0 / (1.0 + jnp.exp(-z))).astype(BF16)


def _inproj(x, w_in, l, rope_c, rope_n, rope_p, gg, gb, bg):
    B, S, D = x.shape
    tm = TM_IN
    tok = lambda width: pl.BlockSpec((1, tm, width), lambda b, i: (b, i, 0))
    lay = lambda width: pl.BlockSpec((None, 1, width), lambda b, i: (l, 0, 0))
    rope = pl.BlockSpec((tm, LANES), lambda b, i: (i, 0))
    outs = [jax.ShapeDtypeStruct((B, S, w), BF16)
            for w in (QK_COLS, QK_COLS, DIFF_WIDTH, GMLP_WIDTH, GMLP_WIDTH, N_BRANCH * D)]
    return pl.pallas_call(
        _inproj_kernel,
        grid=(B, S // tm),
        in_specs=[tok(D),
                  pl.BlockSpec((None, D, IN_COLS), lambda b, i: (l, 0, 0)),
                  rope, rope, rope,
                  lay(GMLP_WIDTH), lay(GMLP_WIDTH), lay(N_BRANCH * D)],
        out_specs=[tok(QK_COLS), tok(QK_COLS), tok(DIFF_WIDTH), tok(GMLP_WIDTH),
                   tok(GMLP_WIDTH), tok(N_BRANCH * D)],
        out_shape=outs,
        compiler_params=_params(2),
        name="inproj",
    )(x, w_in, rope_c, rope_n, rope_p, gg, gb, bg)


def _attn_kernel(lam_init, q_ref, k_ref, v_ref, lq1_ref, lk1_ref, lq2_ref, lk2_ref, sg_ref,
                 o_ref, vt_ref, sa_ref, sb_ref):
    qb, tk = Q_BLOCK, K_TILE
    n_kt = k_ref.shape[1] // tk
    n_qb = q_ref.shape[1] // qb
    lam = (jnp.exp(jnp.sum(lq1_ref[...] * lk1_ref[...], axis=-1, keepdims=True))
           - jnp.exp(jnp.sum(lq2_ref[...] * lk2_ref[...], axis=-1, keepdims=True))
           + lam_init)

    for c in range(v_ref.shape[1] // VT_CHUNK):
        rows = slice(c * VT_CHUNK, (c + 1) * VT_CHUNK)
        vt_ref[:, rows] = v_ref[0, rows, :].astype(F32).T.astype(BF16)

    def block_rows(j):
        start = j * qb if isinstance(j, int) else pl.multiple_of(j * qb, qb)
        return pl.ds(start, qb)

    def stacked_q(j):
        q = q_ref[0, block_rows(j), :]
        lane = lax.broadcasted_iota(jnp.int32, q.shape, 1)
        zero = jnp.zeros_like(q)
        return jnp.concatenate([jnp.where(lane < HEAD_DIM, q, zero),
                                jnp.where(lane >= HEAD_DIM, q, zero)], axis=0)

    def scores_tile(qs, s_ref, t, m, after=None):
        k = k_ref[0, t * tk:(t + 1) * tk, :]
        if after is not None:
            total = sum(after[:, c * LANES:(c + 1) * LANES] for c in range(2 * qb // LANES))
            k = k + jnp.where(total < 0.0, 1.0, 0.0).astype(BF16)
        s = lax.dot_general(k, qs, (((1,), (1,)), ((), ())),
                            preferred_element_type=F32)
        s_ref[t * tk:(t + 1) * tk, :] = s
        return jnp.maximum(m, jnp.max(s, axis=0, keepdims=True))

    def expsum_tile(s_ref, t, m, l, after=None):
        if after is not None:
            total = functools.reduce(
                jnp.maximum, [after[:, c * LANES:(c + 1) * LANES] for c in range(2 * qb // LANES)])
            m = m + jnp.tile(jnp.where(total == jnp.inf, 1.0, 0.0), (1, 2 * qb // LANES))
        e = jnp.exp2(s_ref[t * tk:(t + 1) * tk, :] - m)
        s_ref[t * tk:(t + 1) * tk, :] = e
        return l + jnp.sum(e, axis=0, keepdims=True)

    def output_tile(s_ref, t, r, acc):
        a = (s_ref[t * tk:(t + 1) * tk, :qb]
             - r * s_ref[t * tk:(t + 1) * tk, qb:]).astype(BF16)
        return acc + jnp.dot(vt_ref[:, t * tk:(t + 1) * tk], a, preferred_element_type=F32)

    def stage(j, m, cur_ref, next_ref):
        if next_ref is not None:
            qs = stacked_q(j + 1)
            m_next = jnp.full((1, 2 * qb), -jnp.inf, F32)
        l = jnp.zeros((1, 2 * qb), F32)
        sums, maxes = [], []
        for t in range(n_kt):
            if next_ref is not None:
                after = sums[t - SCORES_LAG] if t >= SCORES_LAG else None
                m_next = scores_tile(qs, next_ref, t, m_next, after)
                maxes.append(m_next)
            after = maxes[t - EXP_LAG] if next_ref is not None and t >= EXP_LAG else None
            l = expsum_tile(cur_ref, t, m, l, after)
            sums.append(l)
        l1, l2 = l[:, :qb], l[:, qb:]
        r = lam * l1 / l2
        acc = jnp.zeros((V_DIM, qb), F32)
        for t in range(n_kt):
            acc = output_tile(cur_ref, t, r, acc)
        o = (acc / l1).T
        ms = jnp.mean(o * o, axis=-1, keepdims=True)
        o = o * lax.rsqrt(ms + RMS_EPS) * sg_ref[...] * (1.0 - lam_init)
        o_ref[0, block_rows(j), :] = o.astype(BF16)
        return m_next if next_ref is not None else m

    qs0 = stacked_q(0)
    m = jnp.full((1, 2 * qb), -jnp.inf, F32)
    for t in range(n_kt):
        m = scores_tile(qs0, sa_ref, t, m)

    def pair(p, m):
        m = stage(2 * p, m, sa_ref, sb_ref)
        return stage(2 * p + 1, m, sb_ref, sa_ref)

    m = lax.fori_loop(0, n_qb // 2 - 1, pair, m)
    m = stage(n_qb - 2, m, sa_ref, sb_ref)
    stage(n_qb - 1, m, sb_ref, None)


def _attention(q, k, v, l, lam_init, lq1, lk1, lq2, lk2, sg):
    B, S, _ = q.shape
    lay = lambda width: pl.BlockSpec((None, 1, width), lambda b, h: (l, 0, 0))
    head = lambda: pl.BlockSpec((1, S, LANES), lambda b, h: (b, 0, h))
    return pl.pallas_call(
        functools.partial(_attn_kernel, lam_init),
        grid=(B, N_HEADS),
        in_specs=[head(), head(), head(),
                  lay(HEAD_DIM), lay(HEAD_DIM), lay(HEAD_DIM), lay(HEAD_DIM), lay(V_DIM)],
        out_specs=head(),
        out_shape=jax.ShapeDtypeStruct((B, S, DIFF_WIDTH), BF16),
        scratch_shapes=[pltpu.VMEM((V_DIM, S), BF16),
                        pltpu.VMEM((S, 2 * Q_BLOCK), F32),
                        pltpu.VMEM((S, 2 * Q_BLOCK), F32)],
        compiler_params=_params(2),
        name="diff_attn",
    )(q, k, v, lq1, lk1, lq2, lk2, sg)


def _merge_kernel(x_ref, a_ref, u_ref, vg_ref, gate_ref, ws_ref, bs_ref, pa_ref, pg_ref,
                  wo_ref, g_ref, b_ref, o_ref, gout_ref):
    for n in range(TM_MERGE // CHUNK):
        rows = slice(n * CHUNK, (n + 1) * CHUNK)
        for g in range(GROUPS):
            cols = slice(g * GROUP_DIM, (g + 1) * GROUP_DIM)
            z = jnp.dot(ws_ref[g], vg_ref[0, rows, cols], preferred_element_type=F32) + bs_ref[g]
            gout_ref[rows, cols] = (u_ref[0, rows, cols].astype(F32) * z).astype(BF16)
    ma = jnp.dot(a_ref[0], pa_ref[...], preferred_element_type=F32)
    mg = jnp.dot(gout_ref[...], pg_ref[...], preferred_element_type=F32)
    mixed = (gate_ref[0, :, :D_MODEL].astype(F32) * ma
             + gate_ref[0, :, D_MODEL:].astype(F32) * mg)
    y = jnp.dot(mixed.astype(BF16), wo_ref[...], preferred_element_type=F32)
    o_ref[0] = _ln(DN_ALPHA * x_ref[0] + y, g_ref[...], b_ref[...])


def _merge(x, a, u, vg, gates, l, ws, bs, pa, pg, wo, g, b):
    B, S, D = x.shape
    tm = TM_MERGE
    tok = lambda width: pl.BlockSpec((1, tm, width), lambda bi, i: (bi, i, 0))
    lay = lambda *shape: pl.BlockSpec((None,) + shape, lambda bi, i: (l,) + (0,) * len(shape))
    return pl.pallas_call(
        _merge_kernel,
        grid=(B, S // tm),
        in_specs=[tok(D), tok(DIFF_WIDTH), tok(GMLP_WIDTH), tok(GMLP_WIDTH), tok(N_BRANCH * D),
                  lay(GROUPS, CHUNK, CHUNK), lay(GROUPS, CHUNK, GROUP_DIM),
                  lay(DIFF_WIDTH, D), lay(GMLP_WIDTH, D), lay(D, D), lay(1, D), lay(1, D)],
        out_specs=tok(D),
        out_shape=jax.ShapeDtypeStruct(x.shape, F32),
        scratch_shapes=[pltpu.VMEM((tm, GMLP_WIDTH), BF16)],
        compiler_params=_params(2),
        name="merge",
    )(x, a, u, vg, gates, ws, bs, pa, pg, wo, g, b)


def _gelu_tanh(t):
    return 0.5 * t * (1.0 + jnp.tanh(math.sqrt(2.0 / math.pi) * (t + 0.044715 * (t * t * t))))


def _ffn_kernel(x_ref, xp_ref, xn_ref, wa_ref, wb_ref, cw_ref, cb_ref, wd_ref, g_ref, b_ref,
                o_ref, xs_ref, a0_ref, a1_ref, g0_ref, g1_ref, h0_ref, h1_ref, acc_ref):
    tm = TM_FFN
    i = pl.program_id(1)
    a_bufs, g_bufs, h_bufs = (a0_ref, a1_ref), (g0_ref, g1_ref), (h0_ref, h1_ref)
    xs_ref[0:HALO] = jnp.where(i > 0, xp_ref[0], 0.0).astype(BF16)
    xs_ref[HALO:HALO + tm] = x_ref[0].astype(BF16)
    xs_ref[HALO + tm:] = jnp.where(i < pl.num_programs(1) - 1, xn_ref[0], 0.0).astype(BF16)

    def up(j):
        a_bufs[j % 2][...] = jnp.dot(xs_ref[...], wa_ref[j], preferred_element_type=F32)
        g_bufs[j % 2][...] = jnp.dot(xs_ref[HALO:HALO + tm], wb_ref[j],
                                     preferred_element_type=F32)

    def act(j):
        a_ref = a_bufs[j % 2]
        cw = cw_ref[j]
        conv = (a_ref[HALO - 1:HALO - 1 + tm] * cw[0:1]
                + a_ref[HALO:HALO + tm] * cw[1:2]
                + a_ref[HALO + 1:HALO + 1 + tm] * cw[2:3] + cb_ref[j])
        h_bufs[j % 2][...] = (_gelu_tanh(conv) * g_bufs[j % 2][...]).astype(BF16)

    def down(j):
        y = jnp.dot(h_bufs[j % 2][...], wd_ref[j], preferred_element_type=F32)
        if j == 0:
            acc_ref[...] = y
        else:
            acc_ref[...] += y

    for s in range(NF + 2):
        if s < NF:
            up(s)
        if 1 <= s <= NF:
            act(s - 1)
        if s >= 2:
            down(s - 2)
    o_ref[0] = _ln(DN_ALPHA * x_ref[0] + acc_ref[...], g_ref[...], b_ref[...])


def _ffn(x, l, wa, wb, cw, cb, wd, g, b):
    B, S, D = x.shape
    tm = TM_FFN
    nh = tm // HALO
    last = S // HALO - 1
    lay = lambda *shape: pl.BlockSpec((None,) + shape, lambda bi, i: (l,) + (0,) * len(shape))
    return pl.pallas_call(
        _ffn_kernel,
        grid=(B, S // tm),
        in_specs=[pl.BlockSpec((1, tm, D), lambda bi, i: (bi, i, 0)),
                  pl.BlockSpec((1, HALO, D), lambda bi, i: (bi, jnp.maximum(i * nh - 1, 0), 0)),
                  pl.BlockSpec((1, HALO, D), lambda bi, i: (bi, jnp.minimum((i + 1) * nh, last), 0)),
                  lay(NF, D, TF), lay(NF, D, TF), lay(NF, 3, TF), lay(NF, 1, TF),
                  lay(NF, TF, D), lay(1, D), lay(1, D)],
        out_specs=pl.BlockSpec((1, tm, D), lambda bi, i: (bi, i, 0)),
        out_shape=jax.ShapeDtypeStruct(x.shape, F32),
        scratch_shapes=[pltpu.VMEM((tm + 2 * HALO, D), BF16),
                        pltpu.VMEM((tm + 2 * HALO, TF), F32),
                        pltpu.VMEM((tm + 2 * HALO, TF), F32),
                        pltpu.VMEM((tm, TF), F32),
                        pltpu.VMEM((tm, TF), F32),
                        pltpu.VMEM((tm, TF), BF16),
                        pltpu.VMEM((tm, TF), BF16),
                        pltpu.VMEM((tm, D), F32)],
        compiler_params=_params(2),
        name="ffn",
    )(x, x, x, wa, wb, cw, cb, wd, g, b)


def _rope_tables(seq):
    inv = 1.0 / (ROPE_THETA ** (jnp.arange(0, ROT_DIM, 2, dtype=F32) / ROT_DIM))
    ang = jnp.arange(seq, dtype=F32)[:, None] * inv[None, :]
    cos, sin = jnp.cos(ang), jnp.sin(ang)
    half = ROT_DIM // 2
    pad = HEAD_DIM - ROT_DIM
    ones = jnp.ones((seq, pad), F32)
    zeros = jnp.zeros((seq, pad), F32)
    zh = jnp.zeros((seq, half), F32)
    c = jnp.concatenate([cos, cos, ones], axis=1)
    s_next = jnp.concatenate([-sin, zh, zeros], axis=1)
    s_prev = jnp.concatenate([zh, sin, zeros], axis=1)
    rep = LANES // HEAD_DIM
    return tuple(jnp.tile(t, (1, rep)) for t in (c, s_next, s_prev))


def kernel(x, ln_in_g, ln_in_b, w_in, b_gate, lambda_q1, lambda_k1, lambda_q2, lambda_k2, subln_g, gmlp_ln_g, gmlp_ln_b, w_spatial, b_spatial, p_attn, p_gmlp, w_o, ln1_g, ln1_b, w_up, conv_w, conv_b, w_down, ln2_g, ln2_b):
    B, S, D = x.shape
    L = DEPTH
    rope_c, rope_n, rope_p = _rope_tables(S)
    row = lambda t: t.reshape(L, 1, -1)
    w_in_b = w_in.astype(BF16)
    ws_b = w_spatial.astype(BF16)
    bs_full = jnp.broadcast_to(b_spatial[..., None], (L, GROUPS, CHUNK, GROUP_DIM))
    pa_b, pg_b, wo_b = p_attn.astype(BF16), p_gmlp.astype(BF16), w_o.astype(BF16)
    split_up = lambda w: w.reshape(L, D, NF, TF).transpose(0, 2, 1, 3).astype(BF16)
    wa_b, wb_b = split_up(w_up[:, :, :D_FF]), split_up(w_up[:, :, D_FF:])
    wd_b = w_down.reshape(L, NF, TF, D).astype(BF16)
    cw_t = conv_w.reshape(L, 3, NF, TF).transpose(0, 2, 1, 3)
    cb_t = conv_b.reshape(L, NF, 1, TF)

    x = _input_ln(x, ln_in_g, ln_in_b)
    for l in range(L):
        lam_init = 0.8 - 0.6 * math.exp(-0.3 * l)
        q, k, v, u, vg, gates = _inproj(x, w_in_b, l, rope_c, rope_n, rope_p,
                                        row(gmlp_ln_g), row(gmlp_ln_b), row(b_gate))
        a = _attention(q, k, v, l, lam_init, row(lambda_q1), row(lambda_k1),
                       row(lambda_q2), row(lambda_k2), row(subln_g))
        x = _merge(x, a, u, vg, gates, l, ws_b, bs_full, pa_b, pg_b, wo_b,
                   row(ln1_g), row(ln1_b))
        x = _ffn(x, l, wa_b, wb_b, cw_t, cb_t, wd_b, row(ln2_g), row(ln2_b))
    return x
```

```python
import functools
import math

import jax
import jax.numpy as jnp
from jax import lax
from jax.experimental import pallas as pl
from jax.experimental.pallas import tpu as pltpu

F32 = jnp.float32
BF16 = jnp.bfloat16

D_MODEL = 1024
DEPTH = 4
N_HEADS = 4
HEAD_DIM = 64
V_DIM = 2 * HEAD_DIM
QK_COLS = N_HEADS * 2 * HEAD_DIM
DIFF_WIDTH = N_HEADS * V_DIM
ROT_DIM = HEAD_DIM // 4
ROPE_THETA = 500000.0
GROUPS = 4
GROUP_DIM = 128
GMLP_WIDTH = GROUPS * GROUP_DIM
CHUNK = 128
N_BRANCH = 2
IN_COLS = 2 * QK_COLS + DIFF_WIDTH + 2 * GMLP_WIDTH + N_BRANCH * D_MODEL
D_FF = 2816
LN_EPS = 1e-5
RMS_EPS = 1e-5
DN_ALPHA = (2 * DEPTH) ** 0.25

LANES = 128
BF16_ROWS = 16
VMEM_LIMIT = 56 * 1024 * 1024

TM_LN = 512
TM_IN = 512
Q_BLOCK = 256
K_TILE = 256
VT_CHUNK = 512
BOUND_SLACK = 1.03
L_MIN = 2.0 ** -64
TINY = 1e-30
TM_MERGE = 512
MERGE_ROWS = 256
TM_FFN = 512
TF = 256
NF = D_FF // TF
HALO = BF16_ROWS

QK_SCALE = HEAD_DIM ** -0.5 * math.log2(math.e)


def _ln(t, g, b):
    mu = jnp.mean(t, axis=-1, keepdims=True)
    d = t - mu
    var = jnp.mean(d * d, axis=-1, keepdims=True)
    return d * lax.rsqrt(var + LN_EPS) * g + b


def _params(n_axes):
    return pltpu.CompilerParams(dimension_semantics=("arbitrary",) * n_axes,
                                vmem_limit_bytes=VMEM_LIMIT)


def _ln_kernel(x_ref, g_ref, b_ref, o_ref):
    o_ref[0] = _ln(x_ref[0], g_ref[...], b_ref[...])


def _input_ln(x, g, b):
    B, S, D = x.shape
    return pl.pallas_call(
        _ln_kernel,
        grid=(B, S // TM_LN),
        in_specs=[pl.BlockSpec((1, TM_LN, D), lambda b, i: (b, i, 0)),
                  pl.BlockSpec((1, D), lambda b, i: (0, 0)),
                  pl.BlockSpec((1, D), lambda b, i: (0, 0))],
        out_specs=pl.BlockSpec((1, TM_LN, D), lambda b, i: (b, i, 0)),
        out_shape=jax.ShapeDtypeStruct(x.shape, F32),
        compiler_params=_params(2),
        name="input_ln",
    )(x, g.reshape(1, D), b.reshape(1, D))


def _rope(t, c, s_next, s_prev):
    return (t * c + pltpu.roll(t, LANES - ROT_DIM // 2, axis=1) * s_next
            + pltpu.roll(t, ROT_DIM // 2, axis=1) * s_prev)


def _inproj_kernel(x_ref, w_ref, rc_ref, rn_ref, rp_ref, gg_ref, gb_ref, bg_ref,
                   q_ref, k_ref, v_ref, u_ref, vg_ref, gate_ref):
    xb = x_ref[0].astype(BF16)
    c, s_next, s_prev = rc_ref[...], rn_ref[...], rp_ref[...]

    def proj(lo, width):
        return jnp.dot(xb, w_ref[:, lo:lo + width], preferred_element_type=F32)

    q = proj(0, QK_COLS)
    k = proj(QK_COLS, QK_COLS)
    for h in range(N_HEADS):
        cols = slice(h * LANES, (h + 1) * LANES)
        q_ref[0, :, cols] = (_rope(q[:, cols], c, s_next, s_prev) * QK_SCALE).astype(BF16)
        k_ref[0, :, cols] = _rope(k[:, cols], c, s_next, s_prev).astype(BF16)
    off = 2 * QK_COLS
    v_ref[0] = proj(off, DIFF_WIDTH).astype(BF16)
    off += DIFF_WIDTH
    u_ref[0] = proj(off, GMLP_WIDTH).astype(BF16)
    off += GMLP_WIDTH
    vg_ref[0] = _ln(proj(off, GMLP_WIDTH), gg_ref[...], gb_ref[...]).astype(BF16)
    off += GMLP_WIDTH
    for j in range(N_BRANCH * D_MODEL // 512):
        z = proj(off + j * 512, 512) + bg_ref[:, j * 512:(j + 1) * 512]
        gate_ref[0, :, j * 512:(j + 1) * 512] = (1.0 / (1.0 + jnp.exp(-z))).astype(BF16)


def _inproj(x, w_in, l, rope_c, rope_n, rope_p, gg, gb, bg):
    B, S, D = x.shape
    tm = TM_IN
    tok = lambda width: pl.BlockSpec((1, tm, width), lambda b, i: (b, i, 0))
    lay = lambda width: pl.BlockSpec((None, 1, width), lambda b, i: (l, 0, 0))
    rope = pl.BlockSpec((tm, LANES), lambda b, i: (i, 0))
    outs = [jax.ShapeDtypeStruct((B, S, w), BF16)
            for w in (QK_COLS, QK_COLS, DIFF_WIDTH, GMLP_WIDTH, GMLP_WIDTH, N_BRANCH * D)]
    return pl.pallas_call(
        _inproj_kernel,
        grid=(B, S // tm),
        in_specs=[tok(D),
                  pl.BlockSpec((None, D, IN_COLS), lambda b, i: (l, 0, 0)),
                  rope, rope, rope,
                  lay(GMLP_WIDTH), lay(GMLP_WIDTH), lay(N_BRANCH * D)],
        out_specs=[tok(QK_COLS), tok(QK_COLS), tok(DIFF_WIDTH), tok(GMLP_WIDTH),
                   tok(GMLP_WIDTH), tok(N_BRANCH * D)],
        out_shape=outs,
        compiler_params=_params(2),
        name="inproj",
    )(x, w_in, rope_c, rope_n, rope_p, gg, gb, bg)


def _attn_kernel(lam_init, q_ref, k_ref, v_ref, lq1_ref, lk1_ref, lq2_ref, lk2_ref, sg_ref,
                 o_ref, vt_ref, ka_ref, kn_ref, qa_ref, ea_ref, eb_ref, la_ref, lb_ref):
    qb, tk = Q_BLOCK, K_TILE
    seq = k_ref.shape[1]
    n_kt = seq // tk
    n_qb = seq // qb
    lam = (jnp.exp(jnp.sum(lq1_ref[...] * lk1_ref[...], axis=-1, keepdims=True))
           - jnp.exp(jnp.sum(lq2_ref[...] * lk2_ref[...], axis=-1, keepdims=True))
           + lam_init)
    lane = lax.broadcasted_iota(jnp.int32, (qb, LANES), 1)
    row = lax.broadcasted_iota(jnp.int32, (LANES, LANES), 0)
    col = lax.broadcasted_iota(jnp.int32, (LANES, LANES), 1)
    ones_sq = jnp.ones((LANES, LANES), BF16)
    comp_sum = jnp.where(row // HEAD_DIM == col, 1.0, 0.0).astype(BF16)

    kmax_sq = jnp.zeros((1, LANES), F32)
    for c in range(seq // VT_CHUNK):
        rows = slice(c * VT_CHUNK, (c + 1) * VT_CHUNK)
        vt_ref[:, rows] = v_ref[0, rows, :].astype(F32).T.astype(BF16)
        k = k_ref[0, rows, :]
        ka_ref[rows, :LANES] = k
        ka_ref[rows, LANES:] = jnp.where(
            lax.broadcasted_iota(jnp.int32, (VT_CHUNK, LANES), 1) == 0, 1.0, 0.0).astype(BF16)
        k_sq = jnp.dot(k * k, comp_sum, preferred_element_type=F32)
        kmax_sq = jnp.maximum(kmax_sq, jnp.max(k_sq, axis=0, keepdims=True))
    kn_ref[:qb] = jnp.broadcast_to(kmax_sq[:, 0:1], (qb, LANES))
    kn_ref[qb:] = jnp.broadcast_to(kmax_sq[:, 1:2], (qb, LANES))

    def block_rows(j):
        start = j * qb if isinstance(j, int) else pl.multiple_of(j * qb, qb)
        return pl.ds(start, qb)

    def stacked_q(j):
        q = q_ref[0, block_rows(j), :]
        zero = jnp.zeros_like(q)
        return jnp.concatenate([jnp.where(lane < HEAD_DIM, q, zero),
                                jnp.where(lane >= HEAD_DIM, q, zero)], axis=0)

    lane2 = lax.broadcasted_iota(jnp.int32, (2 * qb, LANES), 1)
    for j in range(n_qb):
        qs = stacked_q(j)
        q_sq = jnp.dot(qs * qs, ones_sq, preferred_element_type=F32)
        y = q_sq * kn_ref[...] * BOUND_SLACK + TINY
        shift = y * lax.rsqrt(y)
        qa_ref[j * 2 * qb:(j + 1) * 2 * qb, :LANES] = qs
        qa_ref[j * 2 * qb:(j + 1) * 2 * qb, LANES:] = jnp.where(lane2 == 0, -shift, 0.0).astype(BF16)

    def expsum(j, e_ref, l_ref, bad):
        start = j * 2 * qb if isinstance(j, int) else pl.multiple_of(j * 2 * qb, 2 * qb)
        qa = qa_ref[pl.ds(start, 2 * qb), :]
        l = jnp.zeros((1, 2 * qb), F32)
        for t in range(n_kt):
            s = lax.dot_general(ka_ref[t * tk:(t + 1) * tk, :], qa, (((1,), (1,)), ((), ())),
                                preferred_element_type=F32)
            e = jnp.exp2(s)
            e_ref[t * tk:(t + 1) * tk, :] = e
            l = l + jnp.sum(e, axis=0, keepdims=True)
        l_ref[...] = l
        return jnp.maximum(bad, jnp.where(l >= L_MIN, 0.0, 1.0))

    def expsum_exact(j, e_ref, l_ref):
        qs = stacked_q(j)
        m = jnp.full((1, 2 * qb), -jnp.inf, F32)
        for t in range(n_kt):
            s = lax.dot_general(k_ref[0, t * tk:(t + 1) * tk, :], qs, (((1,), (1,)), ((), ())),
                                preferred_element_type=F32)
            e_ref[t * tk:(t + 1) * tk, :] = s
            m = jnp.maximum(m, jnp.max(s, axis=0, keepdims=True))
        l = jnp.zeros((1, 2 * qb), F32)
        for t in range(n_kt):
            e = jnp.exp2(e_ref[t * tk:(t + 1) * tk, :] - m)
            e_ref[t * tk:(t + 1) * tk, :] = e
            l = l + jnp.sum(e, axis=0, keepdims=True)
        l_ref[...] = l

    def output(j, e_ref, l_ref):
        l = l_ref[...]
        l1, l2 = l[:, :qb], l[:, qb:]
        r = lam * l1 / l2
        acc = jnp.zeros((V_DIM, qb), F32)
        for t in range(n_kt):
            a = (e_ref[t * tk:(t + 1) * tk, :qb]
                 - r * e_ref[t * tk:(t + 1) * tk, qb:]).astype(BF16)
            acc = acc + jnp.dot(vt_ref[:, t * tk:(t + 1) * tk], a, preferred_element_type=F32)
        o = (acc / l1).T
        ms = jnp.mean(o * o, axis=-1, keepdims=True)
        o = o * lax.rsqrt(ms + RMS_EPS) * sg_ref[...] * (1.0 - lam_init)
        o_ref[0, block_rows(j), :] = o.astype(BF16)

    def stage(j, cur, nxt, bad):
        bad = expsum(j + 1, *nxt, bad)
        output(j, *cur)
        return bad

    buf_a, buf_b = (ea_ref, la_ref), (eb_ref, lb_ref)
    bad = expsum(0, *buf_a, jnp.zeros((1, 2 * qb), F32))

    def pair(p, bad):
        bad = stage(2 * p, buf_a, buf_b, bad)
        return stage(2 * p + 1, buf_b, buf_a, bad)

    bad = lax.fori_loop(0, n_qb // 2 - 1, pair, bad)
    bad = stage(n_qb - 2, buf_a, buf_b, bad)
    output(n_qb - 1, *buf_b)

    @pl.when(jnp.sum(bad) > 0.0)
    def _():
        def redo(j, carry):
            expsum_exact(j, *buf_a)
            output(j, *buf_a)
            return carry

        lax.fori_loop(0, n_qb, redo, 0)


def _attention(q, k, v, l, lam_init, lq1, lk1, lq2, lk2, sg):
    B, S, _ = q.shape
    lay = lambda width: pl.BlockSpec((None, 1, width), lambda b, h: (l, 0, 0))
    head = lambda: pl.BlockSpec((1, S, LANES), lambda b, h: (b, 0, h))
    return pl.pallas_call(
        functools.partial(_attn_kernel, lam_init),
        grid=(B, N_HEADS),
        in_specs=[head(), head(), head(),
                  lay(HEAD_DIM), lay(HEAD_DIM), lay(HEAD_DIM), lay(HEAD_DIM), lay(V_DIM)],
        out_specs=head(),
        out_shape=jax.ShapeDtypeStruct((B, S, DIFF_WIDTH), BF16),
        scratch_shapes=[pltpu.VMEM((V_DIM, S), BF16),
                        pltpu.VMEM((S, 2 * LANES), BF16),
                        pltpu.VMEM((2 * Q_BLOCK, LANES), F32),
                        pltpu.VMEM((2 * S, 2 * LANES), BF16),
                        pltpu.VMEM((S, 2 * Q_BLOCK), F32),
                        pltpu.VMEM((S, 2 * Q_BLOCK), F32),
                        pltpu.VMEM((1, 2 * Q_BLOCK), F32),
                        pltpu.VMEM((1, 2 * Q_BLOCK), F32)],
        compiler_params=_params(2),
        name="diff_attn",
    )(q, k, v, lq1, lk1, lq2, lk2, sg)


def _merge_kernel(x_ref, a_ref, u_ref, vg_ref, gate_ref, ws_ref, bs_ref, pa_ref, pg_ref,
                  wo_ref, g_ref, b_ref, o_ref, gout_ref):
    for n in range(TM_MERGE // CHUNK):
        rows = slice(n * CHUNK, (n + 1) * CHUNK)
        for g in range(GROUPS):
            cols = slice(g * GROUP_DIM, (g + 1) * GROUP_DIM)
            z = jnp.dot(ws_ref[g], vg_ref[0, rows, cols], preferred_element_type=F32) + bs_ref[g]
            gout_ref[rows, cols] = (u_ref[0, rows, cols].astype(F32) * z).astype(BF16)
    for r in range(TM_MERGE // MERGE_ROWS):
        rows = slice(r * MERGE_ROWS, (r + 1) * MERGE_ROWS)
        ma = jnp.dot(a_ref[0, rows], pa_ref[...], preferred_element_type=F32)
        mg = jnp.dot(gout_ref[rows], pg_ref[...], preferred_element_type=F32)
        mixed = (gate_ref[0, rows, :D_MODEL].astype(F32) * ma
                 + gate_ref[0, rows, D_MODEL:].astype(F32) * mg)
        y = jnp.dot(mixed.astype(BF16), wo_ref[...], preferred_element_type=F32)
        o_ref[0, rows] = _ln(DN_ALPHA * x_ref[0, rows] + y, g_ref[...], b_ref[...])


def _merge(x, a, u, vg, gates, l, ws, bs, pa, pg, wo, g, b):
    B, S, D = x.shape
    tm = TM_MERGE
    tok = lambda width: pl.BlockSpec((1, tm, width), lambda bi, i: (bi, i, 0))
    lay = lambda *shape: pl.BlockSpec((None,) + shape, lambda bi, i: (l,) + (0,) * len(shape))
    return pl.pallas_call(
        _merge_kernel,
        grid=(B, S // tm),
        in_specs=[tok(D), tok(DIFF_WIDTH), tok(GMLP_WIDTH), tok(GMLP_WIDTH), tok(N_BRANCH * D),
                  lay(GROUPS, CHUNK, CHUNK), lay(GROUPS, CHUNK, GROUP_DIM),
                  lay(DIFF_WIDTH, D), lay(GMLP_WIDTH, D), lay(D, D), lay(1, D), lay(1, D)],
        out_specs=tok(D),
        out_shape=jax.ShapeDtypeStruct(x.shape, F32),
        scratch_shapes=[pltpu.VMEM((tm, GMLP_WIDTH), BF16)],
        compiler_params=_params(2),
        name="merge",
    )(x, a, u, vg, gates, ws, bs, pa, pg, wo, g, b)


def _gelu_tanh(t):
    return 0.5 * t * (1.0 + jnp.tanh(math.sqrt(2.0 / math.pi) * (t + 0.044715 * (t * t * t))))


def _ffn_kernel(x_ref, xp_ref, xn_ref, wu_ref, cw_ref, cb_ref, wd_ref, g_ref, b_ref,
                o_ref, xs_ref, a0_ref, a1_ref, g0_ref, g1_ref, h0_ref, h1_ref, acc_ref):
    tm = TM_FFN
    i = pl.program_id(1)
    a_bufs, g_bufs, h_bufs = (a0_ref, a1_ref), (g0_ref, g1_ref), (h0_ref, h1_ref)
    xs_ref[0:HALO] = jnp.where(i > 0, xp_ref[0], 0.0).astype(BF16)
    xs_ref[HALO:HALO + tm] = x_ref[0].astype(BF16)
    xs_ref[HALO + tm:] = jnp.where(i < pl.num_programs(1) - 1, xn_ref[0], 0.0).astype(BF16)

    def up(j):
        a_bufs[j % 2][...] = jnp.dot(xs_ref[...], wu_ref[:, j * TF:(j + 1) * TF],
                                     preferred_element_type=F32)
        g_bufs[j % 2][...] = jnp.dot(xs_ref[HALO:HALO + tm],
                                     wu_ref[:, D_FF + j * TF:D_FF + (j + 1) * TF],
                                     preferred_element_type=F32)

    def act(j):
        a_ref = a_bufs[j % 2]
        cw = cw_ref[j]
        conv = (a_ref[HALO - 1:HALO - 1 + tm] * cw[0:1]
                + a_ref[HALO:HALO + tm] * cw[1:2]
                + a_ref[HALO + 1:HALO + 1 + tm] * cw[2:3] + cb_ref[j])
        h_bufs[j % 2][...] = (_gelu_tanh(conv) * g_bufs[j % 2][...]).astype(BF16)

    def down(j):
        y = jnp.dot(h_bufs[j % 2][...], wd_ref[j], preferred_element_type=F32)
        if j == 0:
            acc_ref[...] = y
        else:
            acc_ref[...] += y

    for s in range(NF + 2):
        if s < NF:
            up(s)
        if 1 <= s <= NF:
            act(s - 1)
        if s >= 2:
            down(s - 2)
    o_ref[0] = _ln(DN_ALPHA * x_ref[0] + acc_ref[...], g_ref[...], b_ref[...])


def _ffn(x, l, wu, cw, cb, wd, g, b):
    B, S, D = x.shape
    tm = TM_FFN
    nh = tm // HALO
    last = S // HALO - 1
    lay = lambda *shape: pl.BlockSpec((None,) + shape, lambda bi, i: (l,) + (0,) * len(shape))
    return pl.pallas_call(
        _ffn_kernel,
        grid=(B, S // tm),
        in_specs=[pl.BlockSpec((1, tm, D), lambda bi, i: (bi, i, 0)),
                  pl.BlockSpec((1, HALO, D), lambda bi, i: (bi, jnp.maximum(i * nh - 1, 0), 0)),
                  pl.BlockSpec((1, HALO, D), lambda bi, i: (bi, jnp.minimum((i + 1) * nh, last), 0)),
                  lay(D, 2 * D_FF), lay(NF, 3, TF), lay(NF, 1, TF),
                  lay(NF, TF, D), lay(1, D), lay(1, D)],
        out_specs=pl.BlockSpec((1, tm, D), lambda bi, i: (bi, i, 0)),
        out_shape=jax.ShapeDtypeStruct(x.shape, F32),
        scratch_shapes=[pltpu.VMEM((tm + 2 * HALO, D), BF16),
                        pltpu.VMEM((tm + 2 * HALO, TF), F32),
                        pltpu.VMEM((tm + 2 * HALO, TF), F32),
                        pltpu.VMEM((tm, TF), F32),
                        pltpu.VMEM((tm, TF), F32),
                        pltpu.VMEM((tm, TF), BF16),
                        pltpu.VMEM((tm, TF), BF16),
                        pltpu.VMEM((tm, D), F32)],
        compiler_params=_params(2),
        name="ffn",
    )(x, x, x, wu, cw, cb, wd, g, b)


def _rope_tables(seq):
    inv = 1.0 / (ROPE_THETA ** (jnp.arange(0, ROT_DIM, 2, dtype=F32) / ROT_DIM))
    ang = jnp.arange(seq, dtype=F32)[:, None] * inv[None, :]
    cos, sin = jnp.cos(ang), jnp.sin(ang)
    half = ROT_DIM // 2
    pad = HEAD_DIM - ROT_DIM
    ones = jnp.ones((seq, pad), F32)
    zeros = jnp.zeros((seq, pad), F32)
    zh = jnp.zeros((seq, half), F32)
    c = jnp.concatenate([cos, cos, ones], axis=1)
    s_next = jnp.concatenate([-sin, zh, zeros], axis=1)
    s_prev = jnp.concatenate([zh, sin, zeros], axis=1)
    rep = LANES // HEAD_DIM
    return tuple(jnp.tile(t, (1, rep)) for t in (c, s_next, s_prev))


def kernel(x, ln_in_g, ln_in_b, w_in, b_gate, lambda_q1, lambda_k1, lambda_q2, lambda_k2, subln_g, gmlp_ln_g, gmlp_ln_b, w_spatial, b_spatial, p_attn, p_gmlp, w_o, ln1_g, ln1_b, w_up, conv_w, conv_b, w_down, ln2_g, ln2_b):
    B, S, D = x.shape
    L = DEPTH
    rope_c, rope_n, rope_p = _rope_tables(S)
    row = lambda t: t.reshape(L, 1, -1)
    w_in_b = w_in.astype(BF16)
    ws_b = w_spatial.astype(BF16)
    bs_full = jnp.broadcast_to(b_spatial[..., None], (L, GROUPS, CHUNK, GROUP_DIM))
    pa_b, pg_b, wo_b = p_attn.astype(BF16), p_gmlp.astype(BF16), w_o.astype(BF16)
    wu_b = w_up.astype(BF16)
    wd_b = w_down.reshape(L, NF, TF, D).astype(BF16)
    cw_t = conv_w.reshape(L, 3, NF, TF).transpose(0, 2, 1, 3)
    cb_t = conv_b.reshape(L, NF, 1, TF)

    x = _input_ln(x, ln_in_g, ln_in_b)
    for l in range(L):
        lam_init = 0.8 - 0.6 * math.exp(-0.3 * l)
        q, k, v, u, vg, gates = _inproj(x, w_in_b, l, rope_c, rope_n, rope_p,
                                        row(gmlp_ln_g), row(gmlp_ln_b), row(b_gate))
        a = _attention(q, k, v, l, lam_init, row(lambda_q1), row(lambda_k1),
                       row(lambda_q2), row(lambda_k2), row(subln_g))
        x = _merge(x, a, u, vg, gates, l, ws_b, bs_full, pa_b, pg_b, wo_b,
                   row(ln1_g), row(ln1_b))
        x = _ffn(x, l, wu_b, cw_t, cb_t, wd_b, row(ln2_g), row(ln2_b))
    return x
```

```python
import functools
import math

import jax
import jax.numpy as jnp
from jax import lax
from jax.experimental import pallas as pl
from jax.experimental.pallas import tpu as pltpu

F32 = jnp.float32
BF16 = jnp.bfloat16

D_MODEL = 1024
DEPTH = 4
N_HEADS = 4
HEAD_DIM = 64
V_DIM = 2 * HEAD_DIM
QK_COLS = N_HEADS * 2 * HEAD_DIM
DIFF_WIDTH = N_HEADS * V_DIM
ROT_DIM = HEAD_DIM // 4
ROPE_THETA = 500000.0
GROUPS = 4
GROUP_DIM = 128
GMLP_WIDTH = GROUPS * GROUP_DIM
CHUNK = 128
N_BRANCH = 2
IN_COLS = 2 * QK_COLS + DIFF_WIDTH + 2 * GMLP_WIDTH + N_BRANCH * D_MODEL
D_FF = 2816
LN_EPS = 1e-5
RMS_EPS = 1e-5
DN_ALPHA = (2 * DEPTH) ** 0.25

LANES = 128
BF16_ROWS = 16
VMEM_LIMIT = 56 * 1024 * 1024
ONE_BUFFER = pl.Buffered(1)

TM_IN = 1024
TM_IN_FIRST = 512
Q_BLOCK = 256
K_TILE = 256
VT_CHUNK = 512
BOUND_SLACK = 1.03
L_MIN = 2.0 ** -64
TINY = 1e-30
TM_MERGE = 1024
MERGE_ROWS = 256
TM_FFN = 512
TF = 256
NF = D_FF // TF
HALO = BF16_ROWS
FFN_SUB = 2
SUB_LAG = 5

QK_SCALE = HEAD_DIM ** -0.5 * math.log2(math.e)


def _ln(t, g, b):
    mu = jnp.mean(t, axis=-1, keepdims=True)
    d = t - mu
    var = jnp.mean(d * d, axis=-1, keepdims=True)
    return d * lax.rsqrt(var + LN_EPS) * g + b


def _params(n_axes):
    return pltpu.CompilerParams(dimension_semantics=("arbitrary",) * n_axes,
                                vmem_limit_bytes=VMEM_LIMIT)


def _rope(t, c, s_next, s_prev):
    return (t * c + pltpu.roll(t, LANES - ROT_DIM // 2, axis=1) * s_next
            + pltpu.roll(t, ROT_DIM // 2, axis=1) * s_prev)


def _inproj_kernel(x_ref, ng_ref, nb_ref, w_ref, rc_ref, rn_ref, rp_ref, gg_ref, gb_ref, bg_ref,
                   q_ref, k_ref, v_ref, u_ref, vg_ref, gate_ref, xn_ref=None):
    x = x_ref[0]
    if xn_ref is not None:
        x = _ln(x, ng_ref[...], nb_ref[...])
        xn_ref[0] = x
    xb = x.astype(BF16)
    c, s_next, s_prev = rc_ref[...], rn_ref[...], rp_ref[...]

    def proj(lo, width):
        return jnp.dot(xb, w_ref[:, lo:lo + width], preferred_element_type=F32)

    q = proj(0, QK_COLS)
    k = proj(QK_COLS, QK_COLS)
    for h in range(N_HEADS):
        cols = slice(h * LANES, (h + 1) * LANES)
        q_ref[0, :, cols] = (_rope(q[:, cols], c, s_next, s_prev) * QK_SCALE).astype(BF16)
        k_ref[0, :, cols] = _rope(k[:, cols], c, s_next, s_prev).astype(BF16)
    off = 2 * QK_COLS
    v_ref[0] = proj(off, DIFF_WIDTH).astype(BF16)
    off += DIFF_WIDTH
    u_ref[0] = proj(off, GMLP_WIDTH).astype(BF16)
    off += GMLP_WIDTH
    vg_ref[0] = _ln(proj(off, GMLP_WIDTH), gg_ref[...], gb_ref[...]).astype(BF16)
    off += GMLP_WIDTH
    for j in range(N_BRANCH * D_MODEL // 512):
        z = proj(off + j * 512, 512) + bg_ref[:, j * 512:(j + 1) * 512]
        gate_ref[0, :, j * 512:(j + 1) * 512] = (1.0 / (1.0 + jnp.exp(-z))).astype(BF16)


def _inproj(x, norm_g, norm_b, w_in, l, rope_c, rope_n, rope_p, gg, gb, bg):
    B, S, D = x.shape
    first = l == 0
    tm = TM_IN_FIRST if first else TM_IN
    tok = lambda width: pl.BlockSpec((1, tm, width), lambda b, i: (b, i, 0))
    lay = lambda width: pl.BlockSpec((None, 1, width), lambda b, i: (l, 0, 0), pipeline_mode=ONE_BUFFER)
    rope = pl.BlockSpec((tm, LANES), lambda b, i: (i, 0))
    outs = [jax.ShapeDtypeStruct((B, S, w), BF16)
            for w in (QK_COLS, QK_COLS, DIFF_WIDTH, GMLP_WIDTH, GMLP_WIDTH, N_BRANCH * D)]
    out_specs = [tok(QK_COLS), tok(QK_COLS), tok(DIFF_WIDTH), tok(GMLP_WIDTH),
                 tok(GMLP_WIDTH), tok(N_BRANCH * D)]
    if first:
        outs.append(jax.ShapeDtypeStruct((B, S, D), F32))
        out_specs.append(tok(D))
    norm = pl.BlockSpec((1, D), lambda b, i: (0, 0), pipeline_mode=ONE_BUFFER)
    return pl.pallas_call(
        _inproj_kernel,
        grid=(B, S // tm),
        in_specs=[tok(D), norm, norm,
                  pl.BlockSpec((None, D, IN_COLS), lambda b, i: (l, 0, 0), pipeline_mode=ONE_BUFFER),
                  rope, rope, rope,
                  lay(GMLP_WIDTH), lay(GMLP_WIDTH), lay(N_BRANCH * D)],
        out_specs=out_specs,
        out_shape=outs,
        compiler_params=_params(2),
        name="inproj",
    )(x, norm_g.reshape(1, D), norm_b.reshape(1, D), w_in, rope_c, rope_n, rope_p, gg, gb, bg)


def _attn_kernel(lam_init, q_ref, k_ref, v_ref, lq1_ref, lk1_ref, lq2_ref, lk2_ref, sg_ref,
                 o_ref, vt_ref, ka_ref, kn_ref, qa_ref, ea_ref, eb_ref, la_ref, lb_ref):
    qb, tk = Q_BLOCK, K_TILE
    seq = k_ref.shape[1]
    n_kt = seq // tk
    n_qb = seq // qb
    lam = (jnp.exp(jnp.sum(lq1_ref[...] * lk1_ref[...], axis=-1, keepdims=True))
           - jnp.exp(jnp.sum(lq2_ref[...] * lk2_ref[...], axis=-1, keepdims=True))
           + lam_init)
    lane = lax.broadcasted_iota(jnp.int32, (qb, LANES), 1)
    row = lax.broadcasted_iota(jnp.int32, (LANES, LANES), 0)
    col = lax.broadcasted_iota(jnp.int32, (LANES, LANES), 1)
    ones_sq = jnp.ones((LANES, LANES), BF16)
    comp_sum = jnp.where(row // HEAD_DIM == col, 1.0, 0.0).astype(BF16)

    kmax_sq = jnp.zeros((1, LANES), F32)
    for c in range(seq // VT_CHUNK):
        rows = slice(c * VT_CHUNK, (c + 1) * VT_CHUNK)
        vt_ref[:, rows] = v_ref[0, rows, :].astype(F32).T.astype(BF16)
        k = k_ref[0, rows, :]
        ka_ref[rows, :LANES] = k
        ka_ref[rows, LANES:] = jnp.where(
            lax.broadcasted_iota(jnp.int32, (VT_CHUNK, LANES), 1) == 0, 1.0, 0.0).astype(BF16)
        k_sq = jnp.dot(k * k, comp_sum, preferred_element_type=F32)
        kmax_sq = jnp.maximum(kmax_sq, jnp.max(k_sq, axis=0, keepdims=True))
    kn_ref[:qb] = jnp.broadcast_to(kmax_sq[:, 0:1], (qb, LANES))
    kn_ref[qb:] = jnp.broadcast_to(kmax_sq[:, 1:2], (qb, LANES))

    def block_rows(j):
        start = j * qb if isinstance(j, int) else pl.multiple_of(j * qb, qb)
        return pl.ds(start, qb)

    def stacked_q(j):
        q = q_ref[0, block_rows(j), :]
        zero = jnp.zeros_like(q)
        return jnp.concatenate([jnp.where(lane < HEAD_DIM, q, zero),
                                jnp.where(lane >= HEAD_DIM, q, zero)], axis=0)

    lane2 = lax.broadcasted_iota(jnp.int32, (2 * qb, LANES), 1)
    for j in range(n_qb):
        qs = stacked_q(j)
        q_sq = jnp.dot(qs * qs, ones_sq, preferred_element_type=F32)
        y = q_sq * kn_ref[...] * BOUND_SLACK + TINY
        shift = y * lax.rsqrt(y)
        qa_ref[j * 2 * qb:(j + 1) * 2 * qb, :LANES] = qs
        qa_ref[j * 2 * qb:(j + 1) * 2 * qb, LANES:] = jnp.where(lane2 == 0, -shift, 0.0).astype(BF16)

    def expsum(j, e_ref, l_ref, bad):
        start = j * 2 * qb if isinstance(j, int) else pl.multiple_of(j * 2 * qb, 2 * qb)
        qa = qa_ref[pl.ds(start, 2 * qb), :]
        l = jnp.zeros((1, 2 * qb), F32)
        for t in range(n_kt):
            s = lax.dot_general(ka_ref[t * tk:(t + 1) * tk, :], qa, (((1,), (1,)), ((), ())),
                                preferred_element_type=F32)
            e = jnp.exp2(s)
            e_ref[t * tk:(t + 1) * tk, :] = e
            l = l + jnp.sum(e, axis=0, keepdims=True)
        l_ref[...] = l
        return jnp.maximum(bad, jnp.where(l >= L_MIN, 0.0, 1.0))

    def expsum_exact(j, e_ref, l_ref):
        qs = stacked_q(j)
        m = jnp.full((1, 2 * qb), -jnp.inf, F32)
        for t in range(n_kt):
            s = lax.dot_general(k_ref[0, t * tk:(t + 1) * tk, :], qs, (((1,), (1,)), ((), ())),
                                preferred_element_type=F32)
            e_ref[t * tk:(t + 1) * tk, :] = s
            m = jnp.maximum(m, jnp.max(s, axis=0, keepdims=True))
        l = jnp.zeros((1, 2 * qb), F32)
        for t in range(n_kt):
            e = jnp.exp2(e_ref[t * tk:(t + 1) * tk, :] - m)
            e_ref[t * tk:(t + 1) * tk, :] = e
            l = l + jnp.sum(e, axis=0, keepdims=True)
        l_ref[...] = l

    def output(j, e_ref, l_ref):
        l = l_ref[...]
        l1, l2 = l[:, :qb], l[:, qb:]
        r = lam * l1 / l2
        acc = jnp.zeros((V_DIM, qb), F32)
        for t in range(n_kt):
            a = (e_ref[t * tk:(t + 1) * tk, :qb]
                 - r * e_ref[t * tk:(t + 1) * tk, qb:]).astype(BF16)
            acc = acc + jnp.dot(vt_ref[:, t * tk:(t + 1) * tk], a, preferred_element_type=F32)
        o = (acc / l1).T
        ms = jnp.mean(o * o, axis=-1, keepdims=True)
        o = o * lax.rsqrt(ms + RMS_EPS) * sg_ref[...] * (1.0 - lam_init)
        o_ref[0, block_rows(j), :] = o.astype(BF16)

    def stage(j, cur, nxt, bad):
        bad = expsum(j + 1, *nxt, bad)
        output(j, *cur)
        return bad

    buf_a, buf_b = (ea_ref, la_ref), (eb_ref, lb_ref)
    bad = expsum(0, *buf_a, jnp.zeros((1, 2 * qb), F32))

    def pair(p, bad):
        bad = stage(2 * p, buf_a, buf_b, bad)
        return stage(2 * p + 1, buf_b, buf_a, bad)

    bad = lax.fori_loop(0, n_qb // 2 - 1, pair, bad)
    bad = stage(n_qb - 2, buf_a, buf_b, bad)
    output(n_qb - 1, *buf_b)

    @pl.when(jnp.sum(bad) > 0.0)
    def _():
        def redo(j, carry):
            expsum_exact(j, *buf_a)
            output(j, *buf_a)
            return carry

        lax.fori_loop(0, n_qb, redo, 0)


def _attention(q, k, v, l, lam_init, lq1, lk1, lq2, lk2, sg):
    B, S, _ = q.shape
    lay = lambda width: pl.BlockSpec((None, 1, width), lambda b, h: (l, 0, 0))
    head = lambda: pl.BlockSpec((1, S, LANES), lambda b, h: (b, 0, h))
    return pl.pallas_call(
        functools.partial(_attn_kernel, lam_init),
        grid=(B, N_HEADS),
        in_specs=[head(), head(), head(),
                  lay(HEAD_DIM), lay(HEAD_DIM), lay(HEAD_DIM), lay(HEAD_DIM), lay(V_DIM)],
        out_specs=head(),
        out_shape=jax.ShapeDtypeStruct((B, S, DIFF_WIDTH), BF16),
        scratch_shapes=[pltpu.VMEM((V_DIM, S), BF16),
                        pltpu.VMEM((S, 2 * LANES), BF16),
                        pltpu.VMEM((2 * Q_BLOCK, LANES), F32),
                        pltpu.VMEM((2 * S, 2 * LANES), BF16),
                        pltpu.VMEM((S, 2 * Q_BLOCK), F32),
                        pltpu.VMEM((S, 2 * Q_BLOCK), F32),
                        pltpu.VMEM((1, 2 * Q_BLOCK), F32),
                        pltpu.VMEM((1, 2 * Q_BLOCK), F32)],
        compiler_params=_params(2),
        name="diff_attn",
    )(q, k, v, lq1, lk1, lq2, lk2, sg)


def _merge_kernel(x_ref, a_ref, u_ref, vg_ref, gate_ref, ws_ref, bs_ref, pa_ref, pg_ref,
                  wo_ref, g_ref, b_ref, o_ref, gout_ref):
    for n in range(TM_MERGE // CHUNK):
        rows = slice(n * CHUNK, (n + 1) * CHUNK)
        for g in range(GROUPS):
            cols = slice(g * GROUP_DIM, (g + 1) * GROUP_DIM)
            z = jnp.dot(ws_ref[g], vg_ref[0, rows, cols], preferred_element_type=F32) + bs_ref[g]
            gout_ref[rows, cols] = (u_ref[0, rows, cols].astype(F32) * z).astype(BF16)
    for r in range(TM_MERGE // MERGE_ROWS):
        rows = slice(r * MERGE_ROWS, (r + 1) * MERGE_ROWS)
        ma = jnp.dot(a_ref[0, rows], pa_ref[...], preferred_element_type=F32)
        mg = jnp.dot(gout_ref[rows], pg_ref[...], preferred_element_type=F32)
        mixed = (gate_ref[0, rows, :D_MODEL].astype(F32) * ma
                 + gate_ref[0, rows, D_MODEL:].astype(F32) * mg)
        y = jnp.dot(mixed.astype(BF16), wo_ref[...], preferred_element_type=F32)
        o_ref[0, rows] = _ln(DN_ALPHA * x_ref[0, rows] + y, g_ref[...], b_ref[...])


def _merge(x, a, u, vg, gates, l, ws, bs, pa, pg, wo, g, b):
    B, S, D = x.shape
    tm = TM_MERGE
    tok = lambda width: pl.BlockSpec((1, tm, width), lambda bi, i: (bi, i, 0))
    lay = lambda *shape: pl.BlockSpec((None,) + shape, lambda bi, i: (l,) + (0,) * len(shape),
                                      pipeline_mode=ONE_BUFFER)
    return pl.pallas_call(
        _merge_kernel,
        grid=(B, S // tm),
        in_specs=[tok(D), tok(DIFF_WIDTH), tok(GMLP_WIDTH), tok(GMLP_WIDTH), tok(N_BRANCH * D),
                  lay(GROUPS, CHUNK, CHUNK), lay(GROUPS, CHUNK, GROUP_DIM),
                  lay(DIFF_WIDTH, D), lay(GMLP_WIDTH, D), lay(D, D), lay(1, D), lay(1, D)],
        out_specs=tok(D),
        out_shape=jax.ShapeDtypeStruct(x.shape, F32),
        scratch_shapes=[pltpu.VMEM((tm, GMLP_WIDTH), BF16)],
        compiler_params=_params(2),
        name="merge",
    )(x, a, u, vg, gates, ws, bs, pa, pg, wo, g, b)


def _gelu_tanh(t):
    return 0.5 * t * (1.0 + jnp.tanh(math.sqrt(2.0 / math.pi) * (t + 0.044715 * (t * t * t))))


def _ffn_kernel(x_ref, xp_ref, xn_ref, wu_ref, cw_ref, cb_ref, wd_ref, g_ref, b_ref,
                o_ref, xs_ref, a_ref, gt_ref, h_ref, acc_ref):
    tm, ts = TM_FFN, TM_FFN // FFN_SUB
    i = pl.program_id(1)
    xs_ref[0:HALO] = jnp.where(i > 0, xp_ref[0], 0.0).astype(BF16)
    xs_ref[HALO:HALO + tm] = x_ref[0].astype(BF16)
    xs_ref[HALO + tm:] = jnp.where(i < pl.num_programs(1) - 1, xn_ref[0], 0.0).astype(BF16)

    def up(t, j):
        a_ref[t, j % 2] = jnp.dot(xs_ref[t * ts:(t + 1) * ts + 2 * HALO],
                                  wu_ref[:, j * TF:(j + 1) * TF], preferred_element_type=F32)
        gt_ref[t, j % 2] = jnp.dot(xs_ref[HALO + t * ts:HALO + (t + 1) * ts],
                                   wu_ref[:, D_FF + j * TF:D_FF + (j + 1) * TF],
                                   preferred_element_type=F32)

    def act(t, j):
        a = a_ref.at[t, j % 2]
        cw = cw_ref[j]
        conv = (a[HALO - 1:HALO - 1 + ts] * cw[0:1]
                + a[HALO:HALO + ts] * cw[1:2]
                + a[HALO + 1:HALO + 1 + ts] * cw[2:3] + cb_ref[j])
        h_ref[t, j % 2] = (_gelu_tanh(conv) * gt_ref[t, j % 2]).astype(BF16)

    def down(t, j):
        rows = slice(t * ts, (t + 1) * ts)
        y = jnp.dot(h_ref[t, j % 2], wd_ref[j], preferred_element_type=F32)
        if j == 0:
            acc_ref[rows] = y
        elif j < NF - 1:
            acc_ref[rows] += y
        else:
            o_ref[0, rows] = _ln(DN_ALPHA * x_ref[0, rows] + (acc_ref[rows] + y),
                                 g_ref[...], b_ref[...])

    def step(t, s):
        if 2 <= s:
            down(t, s - 2)
        if s < NF:
            up(t, s)
        if 1 <= s <= NF:
            act(t, s - 1)

    for s in range(NF + 2 + (FFN_SUB - 1) * SUB_LAG):
        for t in range(FFN_SUB):
            if 0 <= s - t * SUB_LAG < NF + 2:
                step(t, s - t * SUB_LAG)


def _ffn(x, l, wu, cw, cb, wd, g, b):
    B, S, D = x.shape
    tm, ts = TM_FFN, TM_FFN // FFN_SUB
    nh = tm // HALO
    last = S // HALO - 1
    lay = lambda *shape: pl.BlockSpec((None,) + shape, lambda bi, i: (l,) + (0,) * len(shape),
                                      pipeline_mode=ONE_BUFFER)
    return pl.pallas_call(
        _ffn_kernel,
        grid=(B, S // tm),
        in_specs=[pl.BlockSpec((1, tm, D), lambda bi, i: (bi, i, 0)),
                  pl.BlockSpec((1, HALO, D), lambda bi, i: (bi, jnp.maximum(i * nh - 1, 0), 0)),
                  pl.BlockSpec((1, HALO, D), lambda bi, i: (bi, jnp.minimum((i + 1) * nh, last), 0)),
                  lay(D, 2 * D_FF), lay(NF, 3, TF), lay(NF, 1, TF),
                  lay(NF, TF, D), lay(1, D), lay(1, D)],
        out_specs=pl.BlockSpec((1, tm, D), lambda bi, i: (bi, i, 0)),
        out_shape=jax.ShapeDtypeStruct(x.shape, F32),
        scratch_shapes=[pltpu.VMEM((tm + 2 * HALO, D), BF16),
                        pltpu.VMEM((FFN_SUB, 2, ts + 2 * HALO, TF), F32),
                        pltpu.VMEM((FFN_SUB, 2, ts, TF), F32),
                        pltpu.VMEM((FFN_SUB, 2, ts, TF), BF16),
                        pltpu.VMEM((tm, D), F32)],
        compiler_params=_params(2),
        name="ffn",
    )(x, x, x, wu, cw, cb, wd, g, b)


def _rope_tables(seq):
    inv = 1.0 / (ROPE_THETA ** (jnp.arange(0, ROT_DIM, 2, dtype=F32) / ROT_DIM))
    ang = jnp.arange(seq, dtype=F32)[:, None] * inv[None, :]
    cos, sin = jnp.cos(ang), jnp.sin(ang)
    half = ROT_DIM // 2
    pad = HEAD_DIM - ROT_DIM
    ones = jnp.ones((seq, pad), F32)
    zeros = jnp.zeros((seq, pad), F32)
    zh = jnp.zeros((seq, half), F32)
    c = jnp.concatenate([cos, cos, ones], axis=1)
    s_next = jnp.concatenate([-sin, zh, zeros], axis=1)
    s_prev = jnp.concatenate([zh, sin, zeros], axis=1)
    rep = LANES // HEAD_DIM
    return tuple(jnp.tile(t, (1, rep)) for t in (c, s_next, s_prev))


def kernel(x, ln_in_g, ln_in_b, w_in, b_gate, lambda_q1, lambda_k1, lambda_q2, lambda_k2, subln_g, gmlp_ln_g, gmlp_ln_b, w_spatial, b_spatial, p_attn, p_gmlp, w_o, ln1_g, ln1_b, w_up, conv_w, conv_b, w_down, ln2_g, ln2_b):
    B, S, D = x.shape
    L = DEPTH
    rope_c, rope_n, rope_p = _rope_tables(S)
    row = lambda t: t.reshape(L, 1, -1)
    w_in_b = w_in.astype(BF16)
    ws_b = w_spatial.astype(BF16)
    bs_full = jnp.broadcast_to(b_spatial[..., None], (L, GROUPS, CHUNK, GROUP_DIM))
    pa_b, pg_b, wo_b = p_attn.astype(BF16), p_gmlp.astype(BF16), w_o.astype(BF16)
    wu_b = w_up.astype(BF16)
    wd_b = w_down.reshape(L, NF, TF, D).astype(BF16)
    cw_t = conv_w.reshape(L, 3, NF, TF).transpose(0, 2, 1, 3)
    cb_t = conv_b.reshape(L, NF, 1, TF)

    for l in range(L):
        lam_init = 0.8 - 0.6 * math.exp(-0.3 * l)
        q, k, v, u, vg, gates, *normed = _inproj(x, ln_in_g, ln_in_b, w_in_b, l, rope_c, rope_n,
                                                 rope_p, row(gmlp_ln_g), row(gmlp_ln_b),
                                                 row(b_gate))
        if normed:
            x = normed[0]
        a = _attention(q, k, v, l, lam_init, row(lambda_q1), row(lambda_k1),
                       row(lambda_q2), row(lambda_k2), row(subln_g))
        x = _merge(x, a, u, vg, gates, l, ws_b, bs_full, pa_b, pg_b, wo_b,
                   row(ln1_g), row(ln1_b))
        x = _ffn(x, l, wu_b, cw_t, cb_t, wd_b, row(ln2_g), row(ln2_b))
    return x
```

```python
import functools
import math

import jax
import jax.numpy as jnp
from jax import lax
from jax.experimental import pallas as pl
from jax.experimental.pallas import tpu as pltpu

F32 = jnp.float32
BF16 = jnp.bfloat16

D_MODEL = 1024
DEPTH = 4
N_HEADS = 4
HEAD_DIM = 64
V_DIM = 2 * HEAD_DIM
QK_COLS = N_HEADS * 2 * HEAD_DIM
DIFF_WIDTH = N_HEADS * V_DIM
ROT_DIM = HEAD_DIM // 4
ROPE_THETA = 500000.0
GROUPS = 4
GROUP_DIM = 128
GMLP_WIDTH = GROUPS * GROUP_DIM
CHUNK = 128
N_BRANCH = 2
IN_COLS = 2 * QK_COLS + DIFF_WIDTH + 2 * GMLP_WIDTH + N_BRANCH * D_MODEL
D_FF = 2816
LN_EPS = 1e-5
RMS_EPS = 1e-5
DN_ALPHA = (2 * DEPTH) ** 0.25

LANES = 128
BF16_ROWS = 16
VMEM_LIMIT = 56 * 1024 * 1024
ONE_BUFFER = pl.Buffered(1)

TM_IN = 1024
TM_IN_FIRST = 512
Q_BLOCK = 256
K_TILE = 256
VT_CHUNK = 512
BOUND_SLACK = 1.03
L_MIN = 2.0 ** -64
TINY = 1e-30
TM_MERGE = 1024
MERGE_ROWS = 256
TM_FFN = 512
TF = 256
NF = D_FF // TF
HALO = BF16_ROWS

QK_SCALE = HEAD_DIM ** -0.5 * math.log2(math.e)


def _ln(t, g, b):
    mu = jnp.mean(t, axis=-1, keepdims=True)
    d = t - mu
    var = jnp.mean(d * d, axis=-1, keepdims=True)
    return d * lax.rsqrt(var + LN_EPS) * g + b


def _params(n_axes):
    return pltpu.CompilerParams(dimension_semantics=("arbitrary",) * n_axes,
                                vmem_limit_bytes=VMEM_LIMIT)


def _rope(t, c, s_next, s_prev):
    return (t * c + pltpu.roll(t, LANES - ROT_DIM // 2, axis=1) * s_next
            + pltpu.roll(t, ROT_DIM // 2, axis=1) * s_prev)


def _inproj_kernel(x_ref, ng_ref, nb_ref, w_ref, rc_ref, rn_ref, rp_ref, gg_ref, gb_ref, bg_ref,
                   q_ref, k_ref, v_ref, u_ref, vg_ref, gate_ref, xn_ref=None):
    x = x_ref[0]
    if xn_ref is not None:
        x = _ln(x, ng_ref[...], nb_ref[...])
        xn_ref[0] = x
    xb = x.astype(BF16)
    c, s_next, s_prev = rc_ref[...], rn_ref[...], rp_ref[...]

    def proj(lo, width):
        return jnp.dot(xb, w_ref[:, lo:lo + width], preferred_element_type=F32)

    q = proj(0, QK_COLS)
    k = proj(QK_COLS, QK_COLS)
    for h in range(N_HEADS):
        cols = slice(h * LANES, (h + 1) * LANES)
        q_ref[0, :, cols] = (_rope(q[:, cols], c, s_next, s_prev) * QK_SCALE).astype(BF16)
        k_ref[0, :, cols] = _rope(k[:, cols], c, s_next, s_prev).astype(BF16)
    off = 2 * QK_COLS
    v_ref[0] = proj(off, DIFF_WIDTH).astype(BF16)
    off += DIFF_WIDTH
    u_ref[0] = proj(off, GMLP_WIDTH).astype(BF16)
    off += GMLP_WIDTH
    vg_ref[0] = _ln(proj(off, GMLP_WIDTH), gg_ref[...], gb_ref[...]).astype(BF16)
    off += GMLP_WIDTH
    for j in range(N_BRANCH * D_MODEL // 512):
        z = proj(off + j * 512, 512) + bg_ref[:, j * 512:(j + 1) * 512]
        gate_ref[0, :, j * 512:(j + 1) * 512] = (1.0 / (1.0 + jnp.exp(-z))).astype(BF16)


def _inproj(x, norm_g, norm_b, w_in, l, rope_c, rope_n, rope_p, gg, gb, bg):
    B, S, D = x.shape
    first = l == 0
    tm = TM_IN_FIRST if first else TM_IN
    tok = lambda width: pl.BlockSpec((1, tm, width), lambda b, i: (b, i, 0))
    lay = lambda width: pl.BlockSpec((None, 1, width), lambda b, i: (l, 0, 0), pipeline_mode=ONE_BUFFER)
    rope = pl.BlockSpec((tm, LANES), lambda b, i: (i, 0))
    outs = [jax.ShapeDtypeStruct((B, S, w), BF16)
            for w in (QK_COLS, QK_COLS, DIFF_WIDTH, GMLP_WIDTH, GMLP_WIDTH, N_BRANCH * D)]
    out_specs = [tok(QK_COLS), tok(QK_COLS), tok(DIFF_WIDTH), tok(GMLP_WIDTH),
                 tok(GMLP_WIDTH), tok(N_BRANCH * D)]
    if first:
        outs.append(jax.ShapeDtypeStruct((B, S, D), F32))
        out_specs.append(tok(D))
    norm = pl.BlockSpec((1, D), lambda b, i: (0, 0), pipeline_mode=ONE_BUFFER)
    return pl.pallas_call(
        _inproj_kernel,
        grid=(B, S // tm),
        in_specs=[tok(D), norm, norm,
                  pl.BlockSpec((None, D, IN_COLS), lambda b, i: (l, 0, 0), pipeline_mode=ONE_BUFFER),
                  rope, rope, rope,
                  lay(GMLP_WIDTH), lay(GMLP_WIDTH), lay(N_BRANCH * D)],
        out_specs=out_specs,
        out_shape=outs,
        compiler_params=_params(2),
        name="inproj",
    )(x, norm_g.reshape(1, D), norm_b.reshape(1, D), w_in, rope_c, rope_n, rope_p, gg, gb, bg)


def _attn_kernel(lam_init, q_ref, k_ref, v_ref, lq1_ref, lk1_ref, lq2_ref, lk2_ref, sg_ref,
                 o_ref, vt_ref, ka_ref, kn_ref, qa_ref, ea_ref, eb_ref, la_ref, lb_ref):
    qb, tk = Q_BLOCK, K_TILE
    seq = k_ref.shape[1]
    n_kt = seq // tk
    n_qb = seq // qb
    lam = (jnp.exp(jnp.sum(lq1_ref[...] * lk1_ref[...], axis=-1, keepdims=True))
           - jnp.exp(jnp.sum(lq2_ref[...] * lk2_ref[...], axis=-1, keepdims=True))
           + lam_init)
    lane = lax.broadcasted_iota(jnp.int32, (qb, LANES), 1)
    row = lax.broadcasted_iota(jnp.int32, (LANES, LANES), 0)
    col = lax.broadcasted_iota(jnp.int32, (LANES, LANES), 1)
    ones_sq = jnp.ones((LANES, LANES), BF16)
    comp_sum = jnp.where(row // HEAD_DIM == col, 1.0, 0.0).astype(BF16)

    kmax_sq = jnp.zeros((1, LANES), F32)
    for c in range(seq // VT_CHUNK):
        rows = slice(c * VT_CHUNK, (c + 1) * VT_CHUNK)
        vt_ref[:, rows] = v_ref[0, rows, :].astype(F32).T.astype(BF16)
        k = k_ref[0, rows, :]
        ka_ref[rows, :LANES] = k
        ka_ref[rows, LANES:] = jnp.where(
            lax.broadcasted_iota(jnp.int32, (VT_CHUNK, LANES), 1) == 0, 1.0, 0.0).astype(BF16)
        k_sq = jnp.dot(k * k, comp_sum, preferred_element_type=F32)
        kmax_sq = jnp.maximum(kmax_sq, jnp.max(k_sq, axis=0, keepdims=True))
    kn_ref[:qb] = jnp.broadcast_to(kmax_sq[:, 0:1], (qb, LANES))
    kn_ref[qb:] = jnp.broadcast_to(kmax_sq[:, 1:2], (qb, LANES))

    def block_rows(j):
        start = j * qb if isinstance(j, int) else pl.multiple_of(j * qb, qb)
        return pl.ds(start, qb)

    def stacked_q(j):
        q = q_ref[0, block_rows(j), :]
        zero = jnp.zeros_like(q)
        return jnp.concatenate([jnp.where(lane < HEAD_DIM, q, zero),
                                jnp.where(lane >= HEAD_DIM, q, zero)], axis=0)

    def prep_q(j):
        qs = stacked_q(j)
        q_sq = jnp.dot(qs * qs, ones_sq, preferred_element_type=F32)
        y = q_sq * kn_ref[...] * BOUND_SLACK + TINY
        shift = y * lax.rsqrt(y)
        qa_ref[j * 2 * qb:(j + 1) * 2 * qb, :LANES] = qs
        qa_ref[j * 2 * qb:(j + 1) * 2 * qb, LANES:] = (-shift).astype(BF16)

    def expsum(j, e_ref, l_ref, bad):
        start = j * 2 * qb if isinstance(j, int) else pl.multiple_of(j * 2 * qb, 2 * qb)
        qa = qa_ref[pl.ds(start, 2 * qb), :]
        l = jnp.zeros((1, 2 * qb), F32)
        for t in range(n_kt):
            s = lax.dot_general(ka_ref[t * tk:(t + 1) * tk, :], qa, (((1,), (1,)), ((), ())),
                                preferred_element_type=F32)
            e = jnp.exp2(s)
            e_ref[t * tk:(t + 1) * tk, :] = e
            l = l + jnp.sum(e, axis=0, keepdims=True)
        l_ref[...] = l
        return jnp.maximum(bad, jnp.where(l >= L_MIN, 0.0, 1.0))

    def expsum_exact(j, e_ref, l_ref):
        qs = stacked_q(j)
        m = jnp.full((1, 2 * qb), -jnp.inf, F32)
        for t in range(n_kt):
            s = lax.dot_general(k_ref[0, t * tk:(t + 1) * tk, :], qs, (((1,), (1,)), ((), ())),
                                preferred_element_type=F32)
            e_ref[t * tk:(t + 1) * tk, :] = s
            m = jnp.maximum(m, jnp.max(s, axis=0, keepdims=True))
        l = jnp.zeros((1, 2 * qb), F32)
        for t in range(n_kt):
            e = jnp.exp2(e_ref[t * tk:(t + 1) * tk, :] - m)
            e_ref[t * tk:(t + 1) * tk, :] = e
            l = l + jnp.sum(e, axis=0, keepdims=True)
        l_ref[...] = l

    def output(j, e_ref, l_ref):
        l = l_ref[...]
        l1, l2 = l[:, :qb], l[:, qb:]
        r = lam * l1 / l2
        acc = jnp.zeros((V_DIM, qb), F32)
        for t in range(n_kt):
            a = (e_ref[t * tk:(t + 1) * tk, :qb]
                 - r * e_ref[t * tk:(t + 1) * tk, qb:]).astype(BF16)
            acc = acc + jnp.dot(vt_ref[:, t * tk:(t + 1) * tk], a, preferred_element_type=F32)
        o = (acc / l1).T
        ms = jnp.mean(o * o, axis=-1, keepdims=True)
        o = o * lax.rsqrt(ms + RMS_EPS) * sg_ref[...] * (1.0 - lam_init)
        o_ref[0, block_rows(j), :] = o.astype(BF16)

    def stage(j, cur, nxt, bad):
        bad = expsum(j + 1, *nxt, bad)
        output(j, *cur)
        return bad

    buf_a, buf_b = (ea_ref, la_ref), (eb_ref, lb_ref)
    for j in range(n_qb):
        prep_q(j)
    bad = expsum(0, *buf_a, jnp.zeros((1, 2 * qb), F32))

    def pair(p, bad):
        bad = stage(2 * p, buf_a, buf_b, bad)
        return stage(2 * p + 1, buf_b, buf_a, bad)

    bad = lax.fori_loop(0, n_qb // 2 - 1, pair, bad)
    bad = stage(n_qb - 2, buf_a, buf_b, bad)
    output(n_qb - 1, *buf_b)

    @pl.when(jnp.sum(bad) > 0.0)
    def _():
        def redo(j, carry):
            expsum_exact(j, *buf_a)
            output(j, *buf_a)
            return carry

        lax.fori_loop(0, n_qb, redo, 0)


def _attention(q, k, v, l, lam_init, lq1, lk1, lq2, lk2, sg):
    B, S, _ = q.shape
    lay = lambda width: pl.BlockSpec((None, 1, width), lambda b, h: (l, 0, 0))
    head = lambda: pl.BlockSpec((1, S, LANES), lambda b, h: (b, 0, h))
    return pl.pallas_call(
        functools.partial(_attn_kernel, lam_init),
        grid=(B, N_HEADS),
        in_specs=[head(), head(), head(),
                  lay(HEAD_DIM), lay(HEAD_DIM), lay(HEAD_DIM), lay(HEAD_DIM), lay(V_DIM)],
        out_specs=head(),
        out_shape=jax.ShapeDtypeStruct((B, S, DIFF_WIDTH), BF16),
        scratch_shapes=[pltpu.VMEM((V_DIM, S), BF16),
                        pltpu.VMEM((S, 2 * LANES), BF16),
                        pltpu.VMEM((2 * Q_BLOCK, LANES), F32),
                        pltpu.VMEM((2 * S, 2 * LANES), BF16),
                        pltpu.VMEM((S, 2 * Q_BLOCK), F32),
                        pltpu.VMEM((S, 2 * Q_BLOCK), F32),
                        pltpu.VMEM((1, 2 * Q_BLOCK), F32),
                        pltpu.VMEM((1, 2 * Q_BLOCK), F32)],
        compiler_params=_params(2),
        name="diff_attn",
    )(q, k, v, lq1, lk1, lq2, lk2, sg)


def _merge_kernel(x_ref, a_ref, u_ref, vg_ref, gate_ref, ws_ref, bs_ref, pa_ref, pg_ref,
                  wo_ref, g_ref, b_ref, o_ref, gout_ref):
    for n in range(TM_MERGE // CHUNK):
        rows = slice(n * CHUNK, (n + 1) * CHUNK)
        for g in range(GROUPS):
            cols = slice(g * GROUP_DIM, (g + 1) * GROUP_DIM)
            z = jnp.dot(ws_ref[g], vg_ref[0, rows, cols], preferred_element_type=F32) + bs_ref[g]
            gout_ref[rows, cols] = (u_ref[0, rows, cols].astype(F32) * z).astype(BF16)
    for r in range(TM_MERGE // MERGE_ROWS):
        rows = slice(r * MERGE_ROWS, (r + 1) * MERGE_ROWS)
        ma = jnp.dot(a_ref[0, rows], pa_ref[...], preferred_element_type=F32)
        mg = jnp.dot(gout_ref[rows], pg_ref[...], preferred_element_type=F32)
        mixed = (gate_ref[0, rows, :D_MODEL].astype(F32) * ma
                 + gate_ref[0, rows, D_MODEL:].astype(F32) * mg)
        y = jnp.dot(mixed.astype(BF16), wo_ref[...], preferred_element_type=F32)
        o_ref[0, rows] = _ln(DN_ALPHA * x_ref[0, rows] + y, g_ref[...], b_ref[...])


def _merge(x, a, u, vg, gates, l, ws, bs, pa, pg, wo, g, b):
    B, S, D = x.shape
    tm = TM_MERGE
    tok = lambda width: pl.BlockSpec((1, tm, width), lambda bi, i: (bi, i, 0))
    lay = lambda *shape: pl.BlockSpec((None,) + shape, lambda bi, i: (l,) + (0,) * len(shape),
                                      pipeline_mode=ONE_BUFFER)
    return pl.pallas_call(
        _merge_kernel,
        grid=(B, S // tm),
        in_specs=[tok(D), tok(DIFF_WIDTH), tok(GMLP_WIDTH), tok(GMLP_WIDTH), tok(N_BRANCH * D),
                  lay(GROUPS, CHUNK, CHUNK), lay(GROUPS, CHUNK, GROUP_DIM),
                  lay(DIFF_WIDTH, D), lay(GMLP_WIDTH, D), lay(D, D), lay(1, D), lay(1, D)],
        out_specs=tok(D),
        out_shape=jax.ShapeDtypeStruct(x.shape, F32),
        scratch_shapes=[pltpu.VMEM((tm, GMLP_WIDTH), BF16)],
        compiler_params=_params(2),
        name="merge",
    )(x, a, u, vg, gates, ws, bs, pa, pg, wo, g, b)


def _gelu_tanh(t):
    return 0.5 * t * (1.0 + jnp.tanh(math.sqrt(2.0 / math.pi) * (t + 0.044715 * (t * t * t))))


def _ffn_kernel(x_ref, xp_ref, xn_ref, wu_ref, cw_ref, cb_ref, wd_ref, g_ref, b_ref,
                o_ref, xs_ref, a0_ref, a1_ref, g0_ref, g1_ref, h0_ref, h1_ref, acc_ref):
    tm = TM_FFN
    i = pl.program_id(1)
    a_bufs, g_bufs, h_bufs = (a0_ref, a1_ref), (g0_ref, g1_ref), (h0_ref, h1_ref)
    xs_ref[0:HALO] = jnp.where(i > 0, xp_ref[0], 0.0).astype(BF16)
    xs_ref[HALO:HALO + tm] = x_ref[0].astype(BF16)
    xs_ref[HALO + tm:] = jnp.where(i < pl.num_programs(1) - 1, xn_ref[0], 0.0).astype(BF16)

    def up(j):
        a_bufs[j % 2][...] = jnp.dot(xs_ref[...], wu_ref[:, j * TF:(j + 1) * TF],
                                     preferred_element_type=F32)
        g_bufs[j % 2][...] = jnp.dot(xs_ref[HALO:HALO + tm],
                                     wu_ref[:, D_FF + j * TF:D_FF + (j + 1) * TF],
                                     preferred_element_type=F32)

    def act(j):
        a_ref = a_bufs[j % 2]
        cw = cw_ref[j]
        conv = (a_ref[HALO - 1:HALO - 1 + tm] * cw[0:1]
                + a_ref[HALO:HALO + tm] * cw[1:2]
                + a_ref[HALO + 1:HALO + 1 + tm] * cw[2:3] + cb_ref[j])
        h_bufs[j % 2][...] = (_gelu_tanh(conv) * g_bufs[j % 2][...]).astype(BF16)

    def down(j):
        y = jnp.dot(h_bufs[j % 2][...], wd_ref[j], preferred_element_type=F32)
        if j == 0:
            acc_ref[...] = y
        else:
            acc_ref[...] += y

    for s in range(NF + 2):
        if s < NF:
            up(s)
        if 1 <= s <= NF:
            act(s - 1)
        if s >= 2:
            down(s - 2)
    o_ref[0] = _ln(DN_ALPHA * x_ref[0] + acc_ref[...], g_ref[...], b_ref[...])


def _ffn(x, l, wu, cw, cb, wd, g, b):
    B, S, D = x.shape
    tm = TM_FFN
    nh = tm // HALO
    last = S // HALO - 1
    lay = lambda *shape: pl.BlockSpec((None,) + shape, lambda bi, i: (l,) + (0,) * len(shape),
                                      pipeline_mode=ONE_BUFFER)
    return pl.pallas_call(
        _ffn_kernel,
        grid=(B, S // tm),
        in_specs=[pl.BlockSpec((1, tm, D), lambda bi, i: (bi, i, 0)),
                  pl.BlockSpec((1, HALO, D), lambda bi, i: (bi, jnp.maximum(i * nh - 1, 0), 0)),
                  pl.BlockSpec((1, HALO, D), lambda bi, i: (bi, jnp.minimum((i + 1) * nh, last), 0)),
                  lay(D, 2 * D_FF), lay(NF, 3, TF), lay(NF, 1, TF),
                  lay(NF, TF, D), lay(1, D), lay(1, D)],
        out_specs=pl.BlockSpec((1, tm, D), lambda bi, i: (bi, i, 0)),
        out_shape=jax.ShapeDtypeStruct(x.shape, F32),
        scratch_shapes=[pltpu.VMEM((tm + 2 * HALO, D), BF16),
                        pltpu.VMEM((tm + 2 * HALO, TF), F32),
                        pltpu.VMEM((tm + 2 * HALO, TF), F32),
                        pltpu.VMEM((tm, TF), F32),
                        pltpu.VMEM((tm, TF), F32),
                        pltpu.VMEM((tm, TF), BF16),
                        pltpu.VMEM((tm, TF), BF16),
                        pltpu.VMEM((tm, D), F32)],
        compiler_params=_params(2),
        name="ffn",
    )(x, x, x, wu, cw, cb, wd, g, b)


def _rope_tables(seq):
    inv = 1.0 / (ROPE_THETA ** (jnp.arange(0, ROT_DIM, 2, dtype=F32) / ROT_DIM))
    ang = jnp.arange(seq, dtype=F32)[:, None] * inv[None, :]
    cos, sin = jnp.cos(ang), jnp.sin(ang)
    half = ROT_DIM // 2
    pad = HEAD_DIM - ROT_DIM
    ones = jnp.ones((seq, pad), F32)
    zeros = jnp.zeros((seq, pad), F32)
    zh = jnp.zeros((seq, half), F32)
    c = jnp.concatenate([cos, cos, ones], axis=1)
    s_next = jnp.concatenate([-sin, zh, zeros], axis=1)
    s_prev = jnp.concatenate([zh, sin, zeros], axis=1)
    rep = LANES // HEAD_DIM
    return tuple(jnp.tile(t, (1, rep)) for t in (c, s_next, s_prev))


def kernel(x, ln_in_g, ln_in_b, w_in, b_gate, lambda_q1, lambda_k1, lambda_q2, lambda_k2, subln_g, gmlp_ln_g, gmlp_ln_b, w_spatial, b_spatial, p_attn, p_gmlp, w_o, ln1_g, ln1_b, w_up, conv_w, conv_b, w_down, ln2_g, ln2_b):
    B, S, D = x.shape
    L = DEPTH
    rope_c, rope_n, rope_p = _rope_tables(S)
    row = lambda t: t.reshape(L, 1, -1)
    w_in_b = w_in.astype(BF16)
    ws_b = w_spatial.astype(BF16)
    bs_full = jnp.broadcast_to(b_spatial[..., None], (L, GROUPS, CHUNK, GROUP_DIM))
    pa_b, pg_b, wo_b = p_attn.astype(BF16), p_gmlp.astype(BF16), w_o.astype(BF16)
    wu_b = w_up.astype(BF16)
    wd_b = w_down.reshape(L, NF, TF, D).astype(BF16)
    cw_t = conv_w.reshape(L, 3, NF, TF).transpose(0, 2, 1, 3)
    cb_t = conv_b.reshape(L, NF, 1, TF)

    for l in range(L):
        lam_init = 0.8 - 0.6 * math.exp(-0.3 * l)
        q, k, v, u, vg, gates, *normed = _inproj(x, ln_in_g, ln_in_b, w_in_b, l, rope_c, rope_n,
                                                 rope_p, row(gmlp_ln_g), row(gmlp_ln_b),
                                                 row(b_gate))
        if normed:
            x = normed[0]
        a = _attention(q, k, v, l, lam_init, row(lambda_q1), row(lambda_k1),
                       row(lambda_q2), row(lambda_k2), row(subln_g))
        x = _merge(x, a, u, vg, gates, l, ws_b, bs_full, pa_b, pg_b, wo_b,
                   row(ln1_g), row(ln1_b))
        x = _ffn(x, l, wu_b, cw_t, cb_t, wd_b, row(ln2_g), row(ln2_b))
    return x
```

```python
import functools
import math

import jax
import jax.numpy as jnp
from jax import lax
from jax.experimental import pallas as pl
from jax.experimental.pallas import tpu as pltpu

F32 = jnp.float32
BF16 = jnp.bfloat16

D_MODEL = 1024
DEPTH = 4
N_HEADS = 4
HEAD_DIM = 64
V_DIM = 2 * HEAD_DIM
QK_COLS = N_HEADS * 2 * HEAD_DIM
DIFF_WIDTH = N_HEADS * V_DIM
ROT_DIM = HEAD_DIM // 4
ROPE_THETA = 500000.0
GROUPS = 4
GROUP_DIM = 128
GMLP_WIDTH = GROUPS * GROUP_DIM
CHUNK = 128
N_BRANCH = 2
IN_COLS = 2 * QK_COLS + DIFF_WIDTH + 2 * GMLP_WIDTH + N_BRANCH * D_MODEL
D_FF = 2816
LN_EPS = 1e-5
RMS_EPS = 1e-5
DN_ALPHA = (2 * DEPTH) ** 0.25

LANES = 128
BF16_ROWS = 16
VMEM_LIMIT = 56 * 1024 * 1024
ONE_BUFFER = pl.Buffered(1)

TM_IN = 1024
TM_IN_FIRST = 512
Q_BLOCK = 256
K_TILE = 256
ATTN_UNROLL = 4
VT_CHUNK = 512
BOUND_SLACK = 1.03
L_MIN = 2.0 ** -64
TINY = 1e-30
TM_MERGE = 1024
MERGE_ROWS = 256
TM_FFN = 512
TF = 256
NF = D_FF // TF
HALO = BF16_ROWS

QK_SCALE = HEAD_DIM ** -0.5 * math.log2(math.e)


def _ln(t, g, b):
    mu = jnp.mean(t, axis=-1, keepdims=True)
    d = t - mu
    var = jnp.mean(d * d, axis=-1, keepdims=True)
    return d * lax.rsqrt(var + LN_EPS) * g + b


def _params(n_axes):
    return pltpu.CompilerParams(dimension_semantics=("arbitrary",) * n_axes,
                                vmem_limit_bytes=VMEM_LIMIT)


def _rope(t, c, s_next, s_prev):
    return (t * c + pltpu.roll(t, LANES - ROT_DIM // 2, axis=1) * s_next
            + pltpu.roll(t, ROT_DIM // 2, axis=1) * s_prev)


def _inproj_kernel(x_ref, ng_ref, nb_ref, w_ref, rc_ref, rn_ref, rp_ref, gg_ref, gb_ref, bg_ref,
                   q_ref, k_ref, v_ref, u_ref, vg_ref, gate_ref, xn_ref=None):
    x = x_ref[0]
    if xn_ref is not None:
        x = _ln(x, ng_ref[...], nb_ref[...])
        xn_ref[0] = x
    xb = x.astype(BF16)
    c, s_next, s_prev = rc_ref[...], rn_ref[...], rp_ref[...]

    def proj(lo, width):
        return jnp.dot(xb, w_ref[:, lo:lo + width], preferred_element_type=F32)

    q = proj(0, QK_COLS)
    k = proj(QK_COLS, QK_COLS)
    for h in range(N_HEADS):
        cols = slice(h * LANES, (h + 1) * LANES)
        q_ref[0, :, cols] = (_rope(q[:, cols], c, s_next, s_prev) * QK_SCALE).astype(BF16)
        k_ref[0, :, cols] = _rope(k[:, cols], c, s_next, s_prev).astype(BF16)
    off = 2 * QK_COLS
    v_ref[0] = proj(off, DIFF_WIDTH).astype(BF16)
    off += DIFF_WIDTH
    u_ref[0] = proj(off, GMLP_WIDTH).astype(BF16)
    off += GMLP_WIDTH
    vg_ref[0] = _ln(proj(off, GMLP_WIDTH), gg_ref[...], gb_ref[...]).astype(BF16)
    off += GMLP_WIDTH
    for j in range(N_BRANCH * D_MODEL // 512):
        z = proj(off + j * 512, 512) + bg_ref[:, j * 512:(j + 1) * 512]
        gate_ref[0, :, j * 512:(j + 1) * 512] = (1.0 / (1.0 + jnp.exp(-z))).astype(BF16)


def _inproj(x, norm_g, norm_b, w_in, l, rope_c, rope_n, rope_p, gg, gb, bg):
    B, S, D = x.shape
    first = l == 0
    tm = TM_IN_FIRST if first else TM_IN
    tok = lambda width: pl.BlockSpec((1, tm, width), lambda b, i: (b, i, 0))
    lay = lambda width: pl.BlockSpec((None, 1, width), lambda b, i: (l, 0, 0), pipeline_mode=ONE_BUFFER)
    rope = pl.BlockSpec((tm, LANES), lambda b, i: (i, 0))
    outs = [jax.ShapeDtypeStruct((B, S, w), BF16)
            for w in (QK_COLS, QK_COLS, DIFF_WIDTH, GMLP_WIDTH, GMLP_WIDTH, N_BRANCH * D)]
    out_specs = [tok(QK_COLS), tok(QK_COLS), tok(DIFF_WIDTH), tok(GMLP_WIDTH),
                 tok(GMLP_WIDTH), tok(N_BRANCH * D)]
    if first:
        outs.append(jax.ShapeDtypeStruct((B, S, D), F32))
        out_specs.append(tok(D))
    norm = pl.BlockSpec((1, D), lambda b, i: (0, 0), pipeline_mode=ONE_BUFFER)
    return pl.pallas_call(
        _inproj_kernel,
        grid=(B, S // tm),
        in_specs=[tok(D), norm, norm,
                  pl.BlockSpec((None, D, IN_COLS), lambda b, i: (l, 0, 0), pipeline_mode=ONE_BUFFER),
                  rope, rope, rope,
                  lay(GMLP_WIDTH), lay(GMLP_WIDTH), lay(N_BRANCH * D)],
        out_specs=out_specs,
        out_shape=outs,
        compiler_params=_params(2),
        name="inproj",
    )(x, norm_g.reshape(1, D), norm_b.reshape(1, D), w_in, rope_c, rope_n, rope_p, gg, gb, bg)


def _attn_kernel(lam_init, q_ref, k_ref, v_ref, lq1_ref, lk1_ref, lq2_ref, lk2_ref, sg_ref,
                 o_ref, vt_ref, ka_ref, kn_ref, qa_ref, ea_ref, eb_ref, la_ref, lb_ref):
    qb, tk = Q_BLOCK, K_TILE
    seq = k_ref.shape[1]
    n_kt = seq // tk
    n_qb = seq // qb
    lam = (jnp.exp(jnp.sum(lq1_ref[...] * lk1_ref[...], axis=-1, keepdims=True))
           - jnp.exp(jnp.sum(lq2_ref[...] * lk2_ref[...], axis=-1, keepdims=True))
           + lam_init)
    lane = lax.broadcasted_iota(jnp.int32, (qb, LANES), 1)
    row = lax.broadcasted_iota(jnp.int32, (LANES, LANES), 0)
    col = lax.broadcasted_iota(jnp.int32, (LANES, LANES), 1)
    ones_sq = jnp.ones((LANES, LANES), BF16)
    comp_sum = jnp.where(row // HEAD_DIM == col, 1.0, 0.0).astype(BF16)

    kmax_sq = jnp.zeros((1, LANES), F32)
    for c in range(seq // VT_CHUNK):
        rows = slice(c * VT_CHUNK, (c + 1) * VT_CHUNK)
        vt_ref[:, rows] = v_ref[0, rows, :].astype(F32).T.astype(BF16)
        k = k_ref[0, rows, :]
        ka_ref[rows, :LANES] = k
        ka_ref[rows, LANES:] = jnp.where(
            lax.broadcasted_iota(jnp.int32, (VT_CHUNK, LANES), 1) == 0, 1.0, 0.0).astype(BF16)
        k_sq = jnp.dot(k * k, comp_sum, preferred_element_type=F32)
        kmax_sq = jnp.maximum(kmax_sq, jnp.max(k_sq, axis=0, keepdims=True))
    kn_ref[:qb] = jnp.broadcast_to(kmax_sq[:, 0:1], (qb, LANES))
    kn_ref[qb:] = jnp.broadcast_to(kmax_sq[:, 1:2], (qb, LANES))

    def block_rows(j):
        start = j * qb if isinstance(j, int) else pl.multiple_of(j * qb, qb)
        return pl.ds(start, qb)

    def stacked_q(j):
        q = q_ref[0, block_rows(j), :]
        zero = jnp.zeros_like(q)
        return jnp.concatenate([jnp.where(lane < HEAD_DIM, q, zero),
                                jnp.where(lane >= HEAD_DIM, q, zero)], axis=0)

    def prep_q(j):
        qs = stacked_q(j)
        q_sq = jnp.dot(qs * qs, ones_sq, preferred_element_type=F32)
        y = q_sq * kn_ref[...] * BOUND_SLACK + TINY
        shift = y * lax.rsqrt(y)
        qa_ref[j * 2 * qb:(j + 1) * 2 * qb, :LANES] = qs
        qa_ref[j * 2 * qb:(j + 1) * 2 * qb, LANES:] = (-shift).astype(BF16)

    def expsum(j, e_ref, l_ref, bad):
        start = j * 2 * qb if isinstance(j, int) else pl.multiple_of(j * 2 * qb, 2 * qb)
        qa = qa_ref[pl.ds(start, 2 * qb), :]
        l = jnp.zeros((1, 2 * qb), F32)
        for t in range(n_kt):
            s = lax.dot_general(ka_ref[t * tk:(t + 1) * tk, :], qa, (((1,), (1,)), ((), ())),
                                preferred_element_type=F32)
            e = jnp.exp2(s)
            e_ref[t * tk:(t + 1) * tk, :] = e
            l = l + jnp.sum(e, axis=0, keepdims=True)
        l_ref[...] = l
        return jnp.maximum(bad, jnp.where(l >= L_MIN, 0.0, 1.0))

    def expsum_exact(j, e_ref, l_ref):
        qs = stacked_q(j)
        m = jnp.full((1, 2 * qb), -jnp.inf, F32)
        for t in range(n_kt):
            s = lax.dot_general(k_ref[0, t * tk:(t + 1) * tk, :], qs, (((1,), (1,)), ((), ())),
                                preferred_element_type=F32)
            e_ref[t * tk:(t + 1) * tk, :] = s
            m = jnp.maximum(m, jnp.max(s, axis=0, keepdims=True))
        l = jnp.zeros((1, 2 * qb), F32)
        for t in range(n_kt):
            e = jnp.exp2(e_ref[t * tk:(t + 1) * tk, :] - m)
            e_ref[t * tk:(t + 1) * tk, :] = e
            l = l + jnp.sum(e, axis=0, keepdims=True)
        l_ref[...] = l

    def output(j, e_ref, l_ref):
        l = l_ref[...]
        l1, l2 = l[:, :qb], l[:, qb:]
        r = lam * l1 / l2
        acc = jnp.zeros((V_DIM, qb), F32)
        for t in range(n_kt):
            a = (e_ref[t * tk:(t + 1) * tk, :qb]
                 - r * e_ref[t * tk:(t + 1) * tk, qb:]).astype(BF16)
            acc = acc + jnp.dot(vt_ref[:, t * tk:(t + 1) * tk], a, preferred_element_type=F32)
        o = (acc / l1).T
        ms = jnp.mean(o * o, axis=-1, keepdims=True)
        o = o * lax.rsqrt(ms + RMS_EPS) * sg_ref[...] * (1.0 - lam_init)
        o_ref[0, block_rows(j), :] = o.astype(BF16)

    def stage(j, cur, nxt, bad):
        bad = expsum(j + 1, *nxt, bad)
        output(j, *cur)
        return bad

    buf_a, buf_b = (ea_ref, la_ref), (eb_ref, lb_ref)
    for j in range(n_qb):
        prep_q(j)
    bad = expsum(0, *buf_a, jnp.zeros((1, 2 * qb), F32))

    bufs = (buf_a, buf_b)

    def stage_at(j, parity, bad):
        return stage(j, bufs[parity], bufs[1 - parity], bad)

    def group(p, bad):
        for u in range(ATTN_UNROLL):
            bad = stage_at(ATTN_UNROLL * p + u, u % 2, bad)
        return bad

    n_stage = n_qb - 1
    n_group = n_stage // ATTN_UNROLL
    bad = lax.fori_loop(0, n_group, group, bad)
    for j in range(n_group * ATTN_UNROLL, n_stage):
        bad = stage_at(j, j % 2, bad)
    output(n_qb - 1, *bufs[(n_qb - 1) % 2])

    @pl.when(jnp.sum(bad) > 0.0)
    def _():
        def redo(j, carry):
            expsum_exact(j, *buf_a)
            output(j, *buf_a)
            return carry

        lax.fori_loop(0, n_qb, redo, 0)


def _attention(q, k, v, l, lam_init, lq1, lk1, lq2, lk2, sg):
    B, S, _ = q.shape
    lay = lambda width: pl.BlockSpec((None, 1, width), lambda b, h: (l, 0, 0))
    head = lambda: pl.BlockSpec((1, S, LANES), lambda b, h: (b, 0, h))
    return pl.pallas_call(
        functools.partial(_attn_kernel, lam_init),
        grid=(B, N_HEADS),
        in_specs=[head(), head(), head(),
                  lay(HEAD_DIM), lay(HEAD_DIM), lay(HEAD_DIM), lay(HEAD_DIM), lay(V_DIM)],
        out_specs=head(),
        out_shape=jax.ShapeDtypeStruct((B, S, DIFF_WIDTH), BF16),
        scratch_shapes=[pltpu.VMEM((V_DIM, S), BF16),
                        pltpu.VMEM((S, 2 * LANES), BF16),
                        pltpu.VMEM((2 * Q_BLOCK, LANES), F32),
                        pltpu.VMEM((2 * S, 2 * LANES), BF16),
                        pltpu.VMEM((S, 2 * Q_BLOCK), F32),
                        pltpu.VMEM((S, 2 * Q_BLOCK), F32),
                        pltpu.VMEM((1, 2 * Q_BLOCK), F32),
                        pltpu.VMEM((1, 2 * Q_BLOCK), F32)],
        compiler_params=_params(2),
        name="diff_attn",
    )(q, k, v, lq1, lk1, lq2, lk2, sg)


def _merge_kernel(x_ref, a_ref, u_ref, vg_ref, gate_ref, ws_ref, bs_ref, pa_ref, pg_ref,
                  wo_ref, g_ref, b_ref, o_ref, gout_ref):
    for n in range(TM_MERGE // CHUNK):
        rows = slice(n * CHUNK, (n + 1) * CHUNK)
        for g in range(GROUPS):
            cols = slice(g * GROUP_DIM, (g + 1) * GROUP_DIM)
            z = jnp.dot(ws_ref[g], vg_ref[0, rows, cols], preferred_element_type=F32) + bs_ref[g]
            gout_ref[rows, cols] = (u_ref[0, rows, cols].astype(F32) * z).astype(BF16)
    for r in range(TM_MERGE // MERGE_ROWS):
        rows = slice(r * MERGE_ROWS, (r + 1) * MERGE_ROWS)
        ma = jnp.dot(a_ref[0, rows], pa_ref[...], preferred_element_type=F32)
        mg = jnp.dot(gout_ref[rows], pg_ref[...], preferred_element_type=F32)
        mixed = (gate_ref[0, rows, :D_MODEL].astype(F32) * ma
                 + gate_ref[0, rows, D_MODEL:].astype(F32) * mg)
        y = jnp.dot(mixed.astype(BF16), wo_ref[...], preferred_element_type=F32)
        o_ref[0, rows] = _ln(DN_ALPHA * x_ref[0, rows] + y, g_ref[...], b_ref[...])


def _merge(x, a, u, vg, gates, l, ws, bs, pa, pg, wo, g, b):
    B, S, D = x.shape
    tm = TM_MERGE
    tok = lambda width: pl.BlockSpec((1, tm, width), lambda bi, i: (bi, i, 0))
    lay = lambda *shape: pl.BlockSpec((None,) + shape, lambda bi, i: (l,) + (0,) * len(shape),
                                      pipeline_mode=ONE_BUFFER)
    return pl.pallas_call(
        _merge_kernel,
        grid=(B, S // tm),
        in_specs=[tok(D), tok(DIFF_WIDTH), tok(GMLP_WIDTH), tok(GMLP_WIDTH), tok(N_BRANCH * D),
                  lay(GROUPS, CHUNK, CHUNK), lay(GROUPS, CHUNK, GROUP_DIM),
                  lay(DIFF_WIDTH, D), lay(GMLP_WIDTH, D), lay(D, D), lay(1, D), lay(1, D)],
        out_specs=tok(D),
        out_shape=jax.ShapeDtypeStruct(x.shape, F32),
        scratch_shapes=[pltpu.VMEM((tm, GMLP_WIDTH), BF16)],
        compiler_params=_params(2),
        name="merge",
    )(x, a, u, vg, gates, ws, bs, pa, pg, wo, g, b)


def _gelu_tanh(t):
    return 0.5 * t * (1.0 + jnp.tanh(math.sqrt(2.0 / math.pi) * (t + 0.044715 * (t * t * t))))


def _ffn_kernel(x_ref, xp_ref, xn_ref, wu_ref, cw_ref, cb_ref, wd_ref, g_ref, b_ref,
                o_ref, xs_ref, a0_ref, a1_ref, g0_ref, g1_ref, h0_ref, h1_ref, acc_ref):
    tm = TM_FFN
    i = pl.program_id(1)
    a_bufs, g_bufs, h_bufs = (a0_ref, a1_ref), (g0_ref, g1_ref), (h0_ref, h1_ref)
    xs_ref[0:HALO] = jnp.where(i > 0, xp_ref[0], 0.0).astype(BF16)
    xs_ref[HALO:HALO + tm] = x_ref[0].astype(BF16)
    xs_ref[HALO + tm:] = jnp.where(i < pl.num_programs(1) - 1, xn_ref[0], 0.0).astype(BF16)

    def up(j):
        a_bufs[j % 2][...] = jnp.dot(xs_ref[...], wu_ref[:, j * TF:(j + 1) * TF],
                                     preferred_element_type=F32)
        g_bufs[j % 2][...] = jnp.dot(xs_ref[HALO:HALO + tm],
                                     wu_ref[:, D_FF + j * TF:D_FF + (j + 1) * TF],
                                     preferred_element_type=F32)

    def act(j):
        a_ref = a_bufs[j % 2]
        cw = cw_ref[j]
        conv = (a_ref[HALO - 1:HALO - 1 + tm] * cw[0:1]
                + a_ref[HALO:HALO + tm] * cw[1:2]
                + a_ref[HALO + 1:HALO + 1 + tm] * cw[2:3] + cb_ref[j])
        h_bufs[j % 2][...] = (_gelu_tanh(conv) * g_bufs[j % 2][...]).astype(BF16)

    def down(j):
        y = jnp.dot(h_bufs[j % 2][...], wd_ref[j], preferred_element_type=F32)
        if j == 0:
            acc_ref[...] = y
        else:
            acc_ref[...] += y

    for s in range(NF + 2):
        if s < NF:
            up(s)
        if 1 <= s <= NF:
            act(s - 1)
        if s >= 2:
            down(s - 2)
    o_ref[0] = _ln(DN_ALPHA * x_ref[0] + acc_ref[...], g_ref[...], b_ref[...])


def _ffn(x, l, wu, cw, cb, wd, g, b):
    B, S, D = x.shape
    tm = TM_FFN
    nh = tm // HALO
    last = S // HALO - 1
    lay = lambda *shape: pl.BlockSpec((None,) + shape, lambda bi, i: (l,) + (0,) * len(shape),
                                      pipeline_mode=ONE_BUFFER)
    return pl.pallas_call(
        _ffn_kernel,
        grid=(B, S // tm),
        in_specs=[pl.BlockSpec((1, tm, D), lambda bi, i: (bi, i, 0)),
                  pl.BlockSpec((1, HALO, D), lambda bi, i: (bi, jnp.maximum(i * nh - 1, 0), 0)),
                  pl.BlockSpec((1, HALO, D), lambda bi, i: (bi, jnp.minimum((i + 1) * nh, last), 0)),
                  lay(D, 2 * D_FF), lay(NF, 3, TF), lay(NF, 1, TF),
                  lay(NF, TF, D), lay(1, D), lay(1, D)],
        out_specs=pl.BlockSpec((1, tm, D), lambda bi, i: (bi, i, 0)),
        out_shape=jax.ShapeDtypeStruct(x.shape, F32),
        scratch_shapes=[pltpu.VMEM((tm + 2 * HALO, D), BF16),
                        pltpu.VMEM((tm + 2 * HALO, TF), F32),
                        pltpu.VMEM((tm + 2 * HALO, TF), F32),
                        pltpu.VMEM((tm, TF), F32),
                        pltpu.VMEM((tm, TF), F32),
                        pltpu.VMEM((tm, TF), BF16),
                        pltpu.VMEM((tm, TF), BF16),
                        pltpu.VMEM((tm, D), F32)],
        compiler_params=_params(2),
        name="ffn",
    )(x, x, x, wu, cw, cb, wd, g, b)


def _rope_tables(seq):
    inv = 1.0 / (ROPE_THETA ** (jnp.arange(0, ROT_DIM, 2, dtype=F32) / ROT_DIM))
    ang = jnp.arange(seq, dtype=F32)[:, None] * inv[None, :]
    cos, sin = jnp.cos(ang), jnp.sin(ang)
    half = ROT_DIM // 2
    pad = HEAD_DIM - ROT_DIM
    ones = jnp.ones((seq, pad), F32)
    zeros = jnp.zeros((seq, pad), F32)
    zh = jnp.zeros((seq, half), F32)
    c = jnp.concatenate([cos, cos, ones], axis=1)
    s_next = jnp.concatenate([-sin, zh, zeros], axis=1)
    s_prev = jnp.concatenate([zh, sin, zeros], axis=1)
    rep = LANES // HEAD_DIM
    return tuple(jnp.tile(t, (1, rep)) for t in (c, s_next, s_prev))


def kernel(x, ln_in_g, ln_in_b, w_in, b_gate, lambda_q1, lambda_k1, lambda_q2, lambda_k2, subln_g, gmlp_ln_g, gmlp_ln_b, w_spatial, b_spatial, p_attn, p_gmlp, w_o, ln1_g, ln1_b, w_up, conv_w, conv_b, w_down, ln2_g, ln2_b):
    B, S, D = x.shape
    L = DEPTH
    rope_c, rope_n, rope_p = _rope_tables(S)
    row = lambda t: t.reshape(L, 1, -1)
    w_in_b = w_in.astype(BF16)
    ws_b = w_spatial.astype(BF16)
    bs_full = jnp.broadcast_to(b_spatial[..., None], (L, GROUPS, CHUNK, GROUP_DIM))
    pa_b, pg_b, wo_b = p_attn.astype(BF16), p_gmlp.astype(BF16), w_o.astype(BF16)
    wu_b = w_up.astype(BF16)
    wd_b = w_down.reshape(L, NF, TF, D).astype(BF16)
    cw_t = conv_w.reshape(L, 3, NF, TF).transpose(0, 2, 1, 3)
    cb_t = conv_b.reshape(L, NF, 1, TF)

    for l in range(L):
        lam_init = 0.8 - 0.6 * math.exp(-0.3 * l)
        q, k, v, u, vg, gates, *normed = _inproj(x, ln_in_g, ln_in_b, w_in_b, l, rope_c, rope_n,
                                                 rope_p, row(gmlp_ln_g), row(gmlp_ln_b),
                                                 row(b_gate))
        if normed:
            x = normed[0]
        a = _attention(q, k, v, l, lam_init, row(lambda_q1), row(lambda_k1),
                       row(lambda_q2), row(lambda_k2), row(subln_g))
        x = _merge(x, a, u, vg, gates, l, ws_b, bs_full, pa_b, pg_b, wo_b,
                   row(ln1_g), row(ln1_b))
        x = _ffn(x, l, wu_b, cw_t, cb_t, wd_b, row(ln2_g), row(ln2_b))
    return x
```
